```python
import numpy as np
import jax
import jax.numpy as jnp
from jax import lax

D_MODEL = 2048
BATCH = 2
SEQ = 16384
DEPTH = 2

GRID_W = 64
CTX_LEN = 256
HEAD_DIM = 64
W_NA = D_MODEL // 4
W_SG = D_MODEL // 4
W_PL = D_MODEL // 4
W_CV = D_MODEL - W_NA - W_SG - W_PL
D_MIX = W_NA + W_SG + W_PL + W_CV
NA_HEADS = W_NA // HEAD_DIM
NA_WIN_R = 8
NA_WIN_C = 16
SG_HEADS = 8
SG_HD = W_SG // SG_HEADS
SG_CHUNK = 128
TA_DIM = 64
TA_QBLOCK = 128
PL_WINDOWS = (2, 4, 8, 16)
PL_GD = W_PL // len(PL_WINDOWS)
CV_KERNEL = 31
FF_DIM = ((8 * D_MODEL // 3 + 255) // 256) * 256
FF_KERNEL = 3
ROPE_BASE = 10000.0
EPS = 1e-6
IN_SPLITS = (W_NA, W_NA, W_NA, W_SG, W_SG, TA_DIM, TA_DIM, TA_DIM, W_PL, W_CV, W_CV)
P_IN = sum(IN_SPLITS)

kernel_name = 'hybrid_parallel_group_dit_block'


def _rms(x):
    xf = x.astype(jnp.float32)
    return (xf * lax.rsqrt(jnp.mean(xf * xf, axis=-1, keepdims=True) + EPS)).astype(x.dtype)


def _modulate(x, shift, scale):
    return _rms(x) * (1.0 + scale) + shift


def _dwconv(x, w):
    k = w.shape[0]
    return lax.conv_general_dilated(
        x, w[:, None, :].astype(x.dtype), window_strides=(1,),
        padding=[((k - 1) // 2, k // 2)],
        dimension_numbers=('NWC', 'WIO', 'NWC'), feature_group_count=x.shape[-1])


def _rope_2d(x):
    L, dh = x.shape[1], x.shape[-1]
    t = jnp.arange(L)
    half = dh // 2
    freqs = ROPE_BASE ** (-jnp.arange(0, half, 2, dtype=jnp.float32) / half)

    def rot(xa, pos):
        ang = pos.astype(jnp.float32)[:, None] * freqs[None, :]
        cos = jnp.cos(ang)[None, :, None, :]
        sin = jnp.sin(ang)[None, :, None, :]
        x1, x2 = jnp.split(xa.astype(jnp.float32), 2, axis=-1)
        return jnp.concatenate([x1 * cos - x2 * sin, x2 * cos + x1 * sin], axis=-1)

    out = jnp.concatenate([rot(x[..., :half], t // GRID_W), rot(x[..., half:], t % GRID_W)], axis=-1)
    return out.astype(x.dtype)


def _dense_attn(q, k, v):
    s = jnp.einsum('bqhd,bkhd->bhqk', q, k).astype(jnp.float32) * (q.shape[-1] ** -0.5)
    p = jax.nn.softmax(s, axis=-1).astype(v.dtype)
    return jnp.einsum('bhqk,bkhd->bqhd', p, v)


def _global_attn_blocks(q, k, v):
    B, L, H, dh = q.shape
    nb = L // TA_QBLOCK
    qb = q.reshape(B, nb, TA_QBLOCK, H, dh).transpose(1, 0, 2, 3, 4)
    out = lax.map(lambda qi: _dense_attn(qi, k, v), qb)
    return out.transpose(1, 0, 2, 3, 4).reshape(B, L, H, dh)


def _neighbourhood_attn(q, k, v, k_ctx, v_ctx, rpb):
    B, L, H, dh = q.shape
    rows = L // GRID_W
    kr = min(NA_WIN_R, rows)
    n_ctx = k_ctx.shape[1]
    q_g = q.reshape(B, rows, GRID_W, H, dh)
    k_g = k.reshape(B, rows, GRID_W, H, dh)
    v_g = v.reshape(B, rows, GRID_W, H, dh)
    qcol = jnp.arange(GRID_W)
    kcol = jnp.arange(GRID_W)
    c0 = jnp.clip(qcol - NA_WIN_C // 2, 0, GRID_W - NA_WIN_C)
    col_ok = (kcol[None, :] >= c0[:, None]) & (kcol[None, :] < c0[:, None] + NA_WIN_C)
    col_idx = jnp.clip(kcol[None, :] - qcol[:, None] + NA_WIN_C - 1, 0, 2 * NA_WIN_C - 2)
    mask = jnp.concatenate([jnp.tile(col_ok, (1, kr)), jnp.ones((GRID_W, n_ctx), dtype=bool)], axis=-1)
    scale = dh ** -0.5

    def row_block(r):
        r0 = jnp.clip(r - kr // 2, 0, rows - kr)
        q_r = q_g[:, r]
        k_r = lax.dynamic_slice_in_dim(k_g, r0, kr, axis=1).reshape(B, kr * GRID_W, H, dh)
        v_r = lax.dynamic_slice_in_dim(v_g, r0, kr, axis=1).reshape(B, kr * GRID_W, H, dh)
        row_idx = r0 + jnp.arange(kr) - r + NA_WIN_R - 1
        bias = rpb[:, row_idx[:, None, None], col_idx[None, :, :]]
        bias = bias.transpose(0, 2, 1, 3).reshape(H, GRID_W, kr * GRID_W)
        bias = jnp.concatenate([bias, jnp.zeros((H, GRID_W, n_ctx), bias.dtype)], axis=-1)
        k_all = jnp.concatenate([k_r, k_ctx], axis=1)
        v_all = jnp.concatenate([v_r, v_ctx], axis=1)
        s = jnp.einsum('bqhd,bkhd->bhqk', q_r, k_all).astype(jnp.float32) * scale + bias.astype(jnp.float32)
        s = jnp.where(mask, s, -jnp.inf)
        p = jax.nn.softmax(s, axis=-1).astype(v.dtype)
        return jnp.einsum('bhqk,bkhd->bqhd', p, v_all)

    out = lax.map(row_block, jnp.arange(rows))
    return out.transpose(1, 0, 2, 3, 4).reshape(B, L, H, dh)


def _spatial_gate(v, ws, bs):
    B, L, _ = v.shape
    n = L // SG_CHUNK
    vh = v.reshape(B, n, SG_CHUNK, SG_HEADS, SG_HD)
    s = jnp.einsum('hij,bnjhc->bnihc', ws, vh) + bs.T[None, None, :, :, None]
    return s.reshape(B, L, W_SG)


def _pool_mixer(z, pl_w, pl_scale):
    B, L, C = z.shape
    zf = z.astype(jnp.float32)
    cs = jnp.concatenate([jnp.zeros((B, 1, C), jnp.float32), jnp.cumsum(zf, axis=1)], axis=1)
    t = jnp.arange(L)
    outs = []
    for g, w in enumerate(PL_WINDOWS):
        sl = slice(g * PL_GD, (g + 1) * PL_GD)
        lo = jnp.clip(t - w // 2, 0, L)
        hi = jnp.clip(t - w // 2 + w, 0, L)
        csg = cs[..., sl]
        mean = (jnp.take(csg, hi, axis=1) - jnp.take(csg, lo, axis=1)) / (hi - lo).astype(jnp.float32)[None, :, None]
        outs.append(jnp.einsum('blc,cd->bld', (mean - zf[..., sl]).astype(z.dtype), pl_w[g]))
    return jnp.concatenate(outs, axis=-1) * pl_scale


def _conformer_conv(a, g, dw_w, dw_b, norm_gain, pw_w, pw_b):
    h = a * jax.nn.sigmoid(g)
    h = _dwconv(h, dw_w) + dw_b
    h = jax.nn.silu(_rms(h) * norm_gain)
    return h @ pw_w + pw_b


def _conv_glu(h, w_up, dw_w, w_down):
    gate, val = jnp.split(h @ w_up, 2, axis=-1)
    return (jax.nn.silu(_dwconv(gate, dw_w)) * val) @ w_down


def _project(z, na_q_gain, na_k_gain, ta_q_gain, ta_k_gain, rotary):
    na_q, na_k, na_v, sg_u, sg_v, ta_q, ta_k, ta_v, pl_in, cv_a, cv_g = jnp.split(
        z, np.cumsum(IN_SPLITS)[:-1], axis=-1)
    B, L = z.shape[:2]
    na_q = _rms(na_q.reshape(B, L, NA_HEADS, HEAD_DIM)) * na_q_gain
    na_k = _rms(na_k.reshape(B, L, NA_HEADS, HEAD_DIM)) * na_k_gain
    na_v = na_v.reshape(B, L, NA_HEADS, HEAD_DIM)
    ta_q = _rms(ta_q.reshape(B, L, 1, TA_DIM)) * ta_q_gain
    ta_k = _rms(ta_k.reshape(B, L, 1, TA_DIM)) * ta_k_gain
    ta_v = ta_v.reshape(B, L, 1, TA_DIM)
    if rotary:
        ta_q = _rope_2d(ta_q)
        ta_k = _rope_2d(ta_k)
    return (na_q, na_k, na_v, ta_q, ta_k, ta_v), (sg_u, sg_v, pl_in, cv_a, cv_g)


def _merge_groups(na_o, ta_o, sg_u, sg_v, pl_in, cv_a, cv_g, sg_v_gain, sg_ws, sg_bs, ta_w_out,
                  pl_w, pl_scale, cv_dw_w, cv_dw_b, cv_norm_gain, cv_pw_w, cv_pw_b):
    B, L = sg_u.shape[:2]
    u = jax.nn.gelu(sg_u)
    v = _rms(jax.nn.gelu(sg_v)) * sg_v_gain
    sg_o = u * (_spatial_gate(v, sg_ws, sg_bs) + ta_o.reshape(B, L, TA_DIM) @ ta_w_out)
    pl_o = _pool_mixer(pl_in, pl_w, pl_scale)
    cv_o = _conformer_conv(cv_a, cv_g, cv_dw_w, cv_dw_b, cv_norm_gain, cv_pw_w, cv_pw_b)
    return jnp.concatenate([na_o.reshape(B, L, W_NA), sg_o, pl_o, cv_o], axis=-1)


def setup_inputs(seed: int = 0) -> dict:
    key = jax.random.key(seed)
    counter = iter(range(1000))

    def nrm(shape, s):
        return jax.random.normal(jax.random.fold_in(key, next(counter)), shape, jnp.float32) * s

    def gain(shape):
        return 1.0 + nrm(shape, 0.02)

    L = DEPTH
    return {
        'x': nrm((BATCH, SEQ, D_MODEL), 1.0),
        'c': nrm((BATCH, D_MODEL), 1.0),
        'ctx': nrm((BATCH, CTX_LEN, D_MODEL), 1.0),
        'c_ctx': nrm((D_MODEL,), 1.0),
        'w_mod': nrm((L, D_MODEL, 6 * D_MODEL), D_MODEL ** -0.5),
        'b_mod': nrm((L, 6 * D_MODEL), 0.01),
        'w_in': nrm((L, D_MODEL, P_IN), D_MODEL ** -0.5),
        'na_q_gain': gain((L, HEAD_DIM)),
        'na_k_gain': gain((L, HEAD_DIM)),
        'na_rpb': nrm((L, NA_HEADS, 2 * NA_WIN_R - 1, 2 * NA_WIN_C - 1), 0.02),
        'sg_v_gain': gain((L, W_SG)),
        'sg_ws': nrm((L, SG_HEADS, SG_CHUNK, SG_CHUNK), SG_CHUNK ** -0.5),
        'sg_bs': gain((L, SG_HEADS, SG_CHUNK)),
        'ta_q_gain': gain((L, TA_DIM)),
        'ta_k_gain': gain((L, TA_DIM)),
        'ta_w_out': nrm((L, TA_DIM, W_SG), TA_DIM ** -0.5),
        'pl_w': nrm((L, len(PL_WINDOWS), PL_GD, PL_GD), PL_GD ** -0.5),
        'pl_scale': gain((L, W_PL)),
        'cv_dw_w': nrm((L, CV_KERNEL, W_CV), CV_KERNEL ** -0.5),
        'cv_dw_b': nrm((L, W_CV), 0.01),
        'cv_norm_gain': gain((L, W_CV)),
        'cv_pw_w': nrm((L, W_CV, W_CV), W_CV ** -0.5),
        'cv_pw_b': nrm((L, W_CV), 0.01),
        'w_out': nrm((L, D_MIX, D_MODEL), D_MIX ** -0.5),
        'ff_w_up': nrm((L, D_MODEL, 2 * FF_DIM), D_MODEL ** -0.5),
        'ff_dw_w': nrm((L, FF_KERNEL, FF_DIM), FF_KERNEL ** -0.5),
        'ff_w_down': nrm((L, FF_DIM, D_MODEL), FF_DIM ** -0.5),
    }


def reference(x, c, ctx, c_ctx, w_mod, b_mod, w_in, na_q_gain, na_k_gain, na_rpb,
              sg_v_gain, sg_ws, sg_bs, ta_q_gain, ta_k_gain, ta_w_out, pl_w, pl_scale,
              cv_dw_w, cv_dw_b, cv_norm_gain, cv_pw_w, cv_pw_b, w_out, ff_w_up, ff_dw_w, ff_w_down):
    xc = ctx
    for l in range(DEPTH):
        last = l == DEPTH - 1
        mx = jnp.split((jax.nn.silu(c) @ w_mod[l] + b_mod[l])[:, None, :], 6, axis=-1)
        mc = jnp.split((jax.nn.silu(c_ctx) @ w_mod[l] + b_mod[l])[None, None, :], 6, axis=-1)
        attn_c, rest_c = _project(_modulate(xc, mc[0], mc[1]) @ w_in[l], na_q_gain[l], na_k_gain[l],
                                  ta_q_gain[l], ta_k_gain[l], rotary=False)
        attn_x, rest_x = _project(_modulate(x, mx[0], mx[1]) @ w_in[l], na_q_gain[l], na_k_gain[l],
                                  ta_q_gain[l], ta_k_gain[l], rotary=True)
        na_qc, na_kc, na_vc, ta_qc, ta_kc, ta_vc = attn_c
        na_qx, na_kx, na_vx, ta_qx, ta_kx, ta_vx = attn_x
        group_params = (sg_v_gain[l], sg_ws[l], sg_bs[l], ta_w_out[l], pl_w[l], pl_scale[l],
                        cv_dw_w[l], cv_dw_b[l], cv_norm_gain[l], cv_pw_w[l], cv_pw_b[l])
        na_ox = _neighbourhood_attn(na_qx, na_kx, na_vx, na_kc, na_vc, na_rpb[l])
        ta_ox = _global_attn_blocks(ta_qx, jnp.concatenate([ta_kx, ta_kc], axis=1),
                                    jnp.concatenate([ta_vx, ta_vc], axis=1))
        yx = _merge_groups(na_ox, ta_ox, *rest_x, *group_params)
        if not last:
            na_oc = _dense_attn(na_qc, na_kc, na_vc)
            ta_oc = _dense_attn(ta_qc, ta_kc, ta_vc)
            yc = _merge_groups(na_oc, ta_oc, *rest_c, *group_params)
            xc = xc + mc[2] * (yc @ w_out[l])
            xc = xc + mc[5] * _conv_glu(_modulate(xc, mc[3], mc[4]), ff_w_up[l], ff_dw_w[l], ff_w_down[l])
        x = x + mx[2] * (yx @ w_out[l])
        x = x + mx[5] * _conv_glu(_modulate(x, mx[3], mx[4]), ff_w_up[l], ff_dw_w[l], ff_w_down[l])
    return x
```

```python
import functools

import numpy as np
import jax
import jax.numpy as jnp
from jax import lax
from jax.experimental import pallas as pl
from jax.experimental.pallas import tpu as pltpu

F32 = jnp.float32
BF16 = jnp.bfloat16

D_MODEL = 2048
GRID_W = 64
HEAD_DIM = 64
W_GRP = D_MODEL // 4
NA_HEADS = W_GRP // HEAD_DIM
NA_WIN_R = 8
NA_WIN_C = 16
NA_QROWS = 4
NA_BAND = NA_WIN_R + NA_QROWS
SG_HEADS = 8
SG_CHUNK = 128
TA_DIM = 64
PL_WINDOWS = (2, 4, 8, 16)
CV_KERNEL = 31
FF_KERNEL = 3
ROPE_BASE = 10000.0
EPS = 1e-6
HALO = 16
IN_TILE = 512
N_IN_TILES = 9
Z_COLS = 8 * IN_TILE
ZB_NA_Q, ZB_NA_K, ZB_NA_V = 0, 4, 8
ZB_TA_Q, ZB_TA_K, ZB_TA_V = 28, 29, 30
ZT_SG_U, ZT_SG_V, ZT_POOL, ZT_CONV = 3, 4, 5, 6
NEG_BIG = -1e30
NT_DIMS = (((1,), (1,)), ((), ()))
VMEM_LIMIT_MB = 56


def _cparams(sem, vmem_mb=VMEM_LIMIT_MB):
    return pltpu.CompilerParams(dimension_semantics=sem, vmem_limit_bytes=vmem_mb * 1024 * 1024)


def _dot(a, b):
    return jnp.dot(a, b, preferred_element_type=F32)


def _dot_nt(a, b):
    return lax.dot_general(a, b, NT_DIMS, preferred_element_type=F32)


def _mod_kernel(c_ref, w_ref, b_ref, o_ref):
    cv = c_ref[...]
    s = cv * jax.nn.sigmoid(cv)
    o_ref[0] = _dot(s.astype(BF16), w_ref[0].astype(BF16)) + b_ref[0]


def _mod_vectors(c, c_ctx, w_mod, b_mod):
    depth, d, n = w_mod.shape
    nb = c.shape[0]
    assert nb + 1 <= 8
    cv = jnp.zeros((8, d), F32).at[:nb].set(c).at[nb].set(c_ctx)
    tn = 1024
    return pl.pallas_call(
        _mod_kernel,
        grid=(depth, n // tn),
        in_specs=[
            pl.BlockSpec((8, d), lambda l, k: (0, 0)),
            pl.BlockSpec((1, d, tn), lambda l, k: (l, 0, k)),
            pl.BlockSpec((1, 1, tn), lambda l, k: (l, 0, k)),
        ],
        out_specs=pl.BlockSpec((1, 8, tn), lambda l, k: (l, 0, k)),
        out_shape=jax.ShapeDtypeStruct((depth, 8, n), F32),
        compiler_params=_cparams(("parallel", "parallel"), 40),
        name="mod_vectors",
    )(cv, w_mod, b_mod.reshape(depth, 1, n))


def _rope(t, cos, sin):
    lane = lax.broadcasted_iota(jnp.int32, t.shape, 1)
    partner = jnp.where((lane % 32) < 16, pltpu.roll(t, 128 - 16, 1), pltpu.roll(t, 16, 1))
    return t * cos + partner * sin


def _inproj_kernel(x_ref, sh_ref, sc_ref, w_ref, smat_ref, gq_ref, gk_ref, gt_ref, cos_ref, sin_ref,
                   z_ref, xn_scr, a_scr):
    j = pl.program_id(1)

    @pl.when(j == 0)
    def _():
        x = x_ref[...]
        ms = jnp.mean(x * x, axis=-1, keepdims=True)
        xn = x * lax.rsqrt(ms + EPS) * (1.0 + sc_ref[0]) + sh_ref[0]
        xn_scr[...] = xn.astype(BF16)

    acc = _dot(xn_scr[...], w_ref[...])

    def head_rms(a, gain):
        sq = a * a
        hi = sq.astype(BF16)
        lo = (sq - hi.astype(F32)).astype(BF16)
        ss = _dot(hi, smat_ref[...]) + _dot(lo, smat_ref[...])
        return a * lax.rsqrt(ss * (1.0 / HEAD_DIM) + EPS) * gain

    @pl.when(j == 0)
    def _():
        z_ref[...] = (head_rms(acc, gq_ref[...]) * (HEAD_DIM ** -0.5)).astype(BF16)

    @pl.when(j == 1)
    def _():
        z_ref[...] = head_rms(acc, gk_ref[...]).astype(BF16)

    @pl.when(jnp.logical_or(j == 2, j == 5))
    def _():
        z_ref[...] = acc.astype(BF16)

    @pl.when(jnp.logical_or(j == 3, j == 4))
    def _():
        z_ref[...] = jax.nn.gelu(acc).astype(BF16)

    @pl.when(j == 6)
    def _():
        a_scr[...] = acc

    @pl.when(j == 7)
    def _():
        z_ref[...] = (a_scr[...] * jax.nn.sigmoid(acc)).astype(BF16)

    @pl.when(j == 8)
    def _():
        def nrm(t, g):
            ms = jnp.sum(t * t, axis=-1, keepdims=True) * (1.0 / TA_DIM)
            return t * lax.rsqrt(ms + EPS) * g
        cos = cos_ref[...]
        sin = sin_ref[...]
        tq = _rope(nrm(acc[:, 0:128], gt_ref[:, 0:128]), cos, sin) * (TA_DIM ** -0.5)
        tk = _rope(nrm(acc[:, 128:256], gt_ref[:, 128:256]), cos, sin)
        z_ref[:, 0:128] = tq.astype(BF16)
        z_ref[:, 128:256] = tk.astype(BF16)
        z_ref[:, 256:512] = acc[:, 256:512].astype(BF16)


def _inproj(x2d, sh, sc, w_r, smat, gq, gk, gt, cos_t, sin_t, *, tpb):
    t, d = x2d.shape
    tm = min(1024, tpb)
    tiles_pb = tpb // tm
    const = lambda i, j: (0, 0)
    return pl.pallas_call(
        _inproj_kernel,
        grid=(t // tm, N_IN_TILES),
        in_specs=[
            pl.BlockSpec((tm, d), lambda i, j: (i, 0)),
            pl.BlockSpec((1, 1, d), lambda i, j: (i // tiles_pb, 0, 0)),
            pl.BlockSpec((1, 1, d), lambda i, j: (i // tiles_pb, 0, 0)),
            pl.BlockSpec((d, IN_TILE), lambda i, j: (0, j)),
            pl.BlockSpec((IN_TILE, IN_TILE), const),
            pl.BlockSpec((1, IN_TILE), const),
            pl.BlockSpec((1, IN_TILE), const),
            pl.BlockSpec((1, 256), const),
            pl.BlockSpec((tm, 128), lambda i, j: (i % tiles_pb, 0)),
            pl.BlockSpec((tm, 128), lambda i, j: (i % tiles_pb, 0)),
        ],
        out_specs=pl.BlockSpec((tm, IN_TILE), lambda i, j: (i, jnp.where(j >= 7, j - 1, j))),
        out_shape=jax.ShapeDtypeStruct((t, Z_COLS), BF16),
        scratch_shapes=[pltpu.VMEM((tm, d), BF16), pltpu.VMEM((tm, IN_TILE), F32)],
        compiler_params=_cparams(("parallel", "arbitrary")),
        name="in_proj",
    )(x2d, sh, sc, w_r, smat, gq, gk, gt, cos_t, sin_t)


def _stack_heads(q):
    lane = lax.broadcasted_iota(jnp.int32, q.shape, 1)
    zero = jnp.zeros_like(q)
    return jnp.concatenate([jnp.where(lane < HEAD_DIM, q, zero), jnp.where(lane >= HEAD_DIM, q, zero)], axis=0)


def _unstack_heads(o2):
    m = o2.shape[0] // 2
    lane = lax.broadcasted_iota(jnp.int32, (m, 128), 1)
    return jnp.where(lane < HEAD_DIM, o2[:m], o2[m:])


def _na_x_kernel(q_ref, k_ref, v_ref, kc_ref, vc_ref, b_ref, o_ref, *, rows):
    rb = pl.program_id(2)
    rs = jnp.clip(rb * NA_QROWS - NA_WIN_R // 2, 0, rows - NA_BAND)
    start = pl.multiple_of(rs * GRID_W, GRID_W)
    nkeys = NA_BAND * GRID_W
    kw = k_ref[pl.ds(start, nkeys), :]
    vw = v_ref[pl.ds(start, nkeys), :]
    q2 = _stack_heads(q_ref[...])
    s1 = _dot_nt(q2, kw) + b_ref[0, 0]
    s2 = _dot_nt(q2, kc_ref[...])
    m = jnp.maximum(jnp.max(s1, axis=-1, keepdims=True), jnp.max(s2, axis=-1, keepdims=True))
    p1 = jnp.exp(s1 - m)
    p2 = jnp.exp(s2 - m)
    l = jnp.sum(p1, axis=-1, keepdims=True) + jnp.sum(p2, axis=-1, keepdims=True)
    o2 = _dot(p1.astype(BF16), vw) + _dot(p2.astype(BF16), vc_ref[...])
    o_ref[...] = _unstack_heads(o2 / l).astype(BF16)


def _na_x(z, zc, bias, *, nb, seq, lc):
    rows = seq // GRID_W
    assert rows % NA_QROWS == 0 and rows >= NA_BAND
    nrb = rows // NA_QROWS
    tq = NA_QROWS * GRID_W
    case = lambda rb: jnp.where(rb == 0, 0, jnp.where(rb == nrb - 1, 2, 1))
    return pl.pallas_call(
        functools.partial(_na_x_kernel, rows=rows),
        grid=(nb, NA_HEADS // 2, nrb),
        in_specs=[
            pl.BlockSpec((tq, 128), lambda b, hp, rb: (b * nrb + rb, ZB_NA_Q + hp)),
            pl.BlockSpec((seq, 128), lambda b, hp, rb: (b, ZB_NA_K + hp)),
            pl.BlockSpec((seq, 128), lambda b, hp, rb: (b, ZB_NA_V + hp)),
            pl.BlockSpec((lc, 128), lambda b, hp, rb: (b, ZB_NA_K + hp)),
            pl.BlockSpec((lc, 128), lambda b, hp, rb: (b, ZB_NA_V + hp)),
            pl.BlockSpec((1, 1, 2 * tq, NA_BAND * GRID_W), lambda b, hp, rb: (case(rb), hp, 0, 0)),
        ],
        out_specs=pl.BlockSpec((tq, 128), lambda b, hp, rb: (b * nrb + rb, hp)),
        out_shape=jax.ShapeDtypeStruct((nb * seq, W_GRP), BF16),
        compiler_params=_cparams(("parallel", "parallel", "arbitrary")),
        name="na_attn",
    )(z, z, z, zc, zc, bias)


def _na_bias_table(rpb):
    nq = NA_QROWS * GRID_W
    nk = NA_BAND * GRID_W
    qr, qc = np.arange(nq) // GRID_W, np.arange(nq) % GRID_W
    wr, kc = np.arange(nk) // GRID_W, np.arange(nk) % GRID_W
    c0 = np.clip(qc - NA_WIN_C // 2, 0, GRID_W - NA_WIN_C)
    col_ok = (kc[None, :] >= c0[:, None]) & (kc[None, :] < c0[:, None] + NA_WIN_C)
    cidx = np.clip(kc[None, :] - qc[:, None] + NA_WIN_C - 1, 0, 2 * NA_WIN_C - 2)
    half = NA_WIN_R // 2
    cases = ((0, np.zeros(NA_QROWS, int)), (half, np.arange(NA_QROWS)), (NA_WIN_R, np.full(NA_QROWS, half)))
    tabs = []
    for dq, r0rel in cases:
        r0 = r0rel[qr]
        row_ok = (wr[None, :] >= r0[:, None]) & (wr[None, :] < r0[:, None] + NA_WIN_R)
        ridx = np.clip(wr[None, :] - (dq + qr[:, None]) + NA_WIN_R - 1, 0, 2 * NA_WIN_R - 2)
        valid = jnp.asarray(row_ok & col_ok)
        tabs.append(jnp.where(valid[None], rpb[:, ridx, cidx], NEG_BIG))
    tab = jnp.stack(tabs)
    h = rpb.shape[0]
    return tab.reshape(3, h // 2, 2 * nq, nk).astype(F32)


def _na_ctx_kernel(q_ref, k_ref, v_ref, o_ref):
    q2 = _stack_heads(q_ref[...])
    s = _dot_nt(q2, k_ref[...])
    m = jnp.max(s, axis=-1, keepdims=True)
    p = jnp.exp(s - m)
    l = jnp.sum(p, axis=-1, keepdims=True)
    o2 = _dot(p.astype(BF16), v_ref[...])
    o_ref[...] = _unstack_heads(o2 / l).astype(BF16)


def _na_ctx(zc, *, nb, lc):
    return pl.pallas_call(
        _na_ctx_kernel,
        grid=(nb, NA_HEADS // 2),
        in_specs=[
            pl.BlockSpec((lc, 128), lambda b, hp: (b, ZB_NA_Q + hp)),
            pl.BlockSpec((lc, 128), lambda b, hp: (b, ZB_NA_K + hp)),
            pl.BlockSpec((lc, 128), lambda b, hp: (b, ZB_NA_V + hp)),
        ],
        out_specs=pl.BlockSpec((lc, 128), lambda b, hp: (b, hp)),
        out_shape=jax.ShapeDtypeStruct((nb * lc, W_GRP), BF16),
        compiler_params=_cparams(("parallel", "parallel")),
        name="na_attn_ctx",
    )(zc, zc, zc)


def _ta_kernel(*refs, tk, nk, has_extra):
    if has_extra:
        q_ref, k_ref, v_ref, ke_ref, ve_ref, o_ref = refs
    else:
        q_ref, k_ref, v_ref, o_ref = refs
    q = q_ref[...]
    tq = q.shape[0]

    def step(kb, vb, carry):
        m, l, acc = carry
        s = _dot_nt(q, kb)
        mn = jnp.maximum(m, jnp.max(s, axis=-1, keepdims=True))
        a = jnp.exp(m - mn)
        p = jnp.exp(s - mn)
        l = a * l + jnp.sum(p, axis=-1, keepdims=True)
        acc = a * acc + _dot(p.astype(BF16), vb)
        return mn, l, acc

    def body(c, carry):
        st = pl.multiple_of(c * tk, tk)
        return step(k_ref[pl.ds(st, tk), :], v_ref[pl.ds(st, tk), :], carry)

    carry = (jnp.full((tq, 1), NEG_BIG, F32), jnp.zeros((tq, 1), F32), jnp.zeros((tq, 128), F32))
    carry = lax.fori_loop(0, nk, body, carry)
    if has_extra:
        carry = step(ke_ref[...], ve_ref[...], carry)
    _, l, acc = carry
    o_ref[...] = (acc / l).astype(BF16)


def _ta(zq, zextra, *, nb, lq, le):
    tq = min(512, lq)
    tk = min(1024, lq)
    nqb = lq // tq
    has_extra = zextra is not None
    in_specs = [
        pl.BlockSpec((tq, 128), lambda b, qi: (b * nqb + qi, ZB_TA_Q)),
        pl.BlockSpec((lq, 128), lambda b, qi: (b, ZB_TA_K)),
        pl.BlockSpec((lq, 128), lambda b, qi: (b, ZB_TA_V)),
    ]
    args = [zq, zq, zq]
    if has_extra:
        in_specs += [
            pl.BlockSpec((le, 128), lambda b, qi: (b, ZB_TA_K)),
            pl.BlockSpec((le, 128), lambda b, qi: (b, ZB_TA_V)),
        ]
        args += [zextra, zextra]
    return pl.pallas_call(
        functools.partial(_ta_kernel, tk=tk, nk=lq // tk, has_extra=has_extra),
        grid=(nb, nqb),
        in_specs=in_specs,
        out_specs=pl.BlockSpec((tq, 128), lambda b, qi: (b * nqb + qi, 0)),
        out_shape=jax.ShapeDtypeStruct((nb * lq, 128), BF16),
        compiler_params=_cparams(("parallel", "arbitrary")),
        name="ta_attn",
    )(*args)


def _sg_kernel(u_ref, gv_ref, tao_ref, gain_ref, wcat_ref, bsx_ref, two_ref, o_ref, v_scr):
    tm = u_ref.shape[0]
    gv = gv_ref[...].astype(F32)
    ms = jnp.mean(gv * gv, axis=-1, keepdims=True)
    v_scr[...] = (gv * lax.rsqrt(ms + EPS) * gain_ref[...]).astype(BF16)
    lane = lax.broadcasted_iota(jnp.int32, (SG_CHUNK, 128), 1)
    zero = jnp.zeros((SG_CHUNK, 128), BF16)

    def chunk(c, carry):
        r0 = pl.multiple_of(c * SG_CHUNK, SG_CHUNK)
        rows = pl.ds(r0, SG_CHUNK)
        outs = []
        for p in range(SG_HEADS // 2):
            vs = v_scr[rows, p * 128:(p + 1) * 128]
            rhs = jnp.concatenate([jnp.where(lane < 64, vs, zero), jnp.where(lane >= 64, vs, zero)], axis=0)
            outs.append(_dot(wcat_ref[p], rhs))
        s = jnp.concatenate(outs, axis=1) + bsx_ref[...] + _dot(tao_ref[rows, :], two_ref[...])
        o_ref[rows, :] = (u_ref[rows, :].astype(F32) * s).astype(BF16)
        return carry

    lax.fori_loop(0, tm // SG_CHUNK, chunk, 0)


def _sg(z, tao, gain, wcat, bsx, two, *, tpb):
    t = z.shape[0]
    tm = min(512, tpb)
    const2 = lambda i: (0, 0)
    return pl.pallas_call(
        _sg_kernel,
        grid=(t // tm,),
        in_specs=[
            pl.BlockSpec((tm, W_GRP), lambda i: (i, ZT_SG_U)),
            pl.BlockSpec((tm, W_GRP), lambda i: (i, ZT_SG_V)),
            pl.BlockSpec((tm, 128), lambda i: (i, 0)),
            pl.BlockSpec((1, W_GRP), const2),
            pl.BlockSpec((SG_HEADS // 2, SG_CHUNK, 2 * SG_CHUNK), lambda i: (0, 0, 0)),
            pl.BlockSpec((SG_CHUNK, W_GRP), const2),
            pl.BlockSpec((128, W_GRP), const2),
        ],
        out_specs=pl.BlockSpec((tm, W_GRP), lambda i: (i, 0)),
        out_shape=jax.ShapeDtypeStruct((t, W_GRP), BF16),
        scratch_shapes=[pltpu.VMEM((tm, W_GRP), BF16)],
        compiler_params=_cparams(("parallel",)),
        name="sg_unit",
    )(z, z, tao, gain, wcat, bsx, two)


def _halo_specs(tm, t, width, col):
    hb = tm // HALO
    last = t // HALO - 1
    return [
        pl.BlockSpec((tm, width), lambda i: (i, col)),
        pl.BlockSpec((HALO, width), lambda i: (jnp.maximum(i * hb - 1, 0), col)),
        pl.BlockSpec((HALO, width), lambda i: (jnp.minimum((i + 1) * hb, last), col)),
    ]


def _fill_ext(ext_scr, main_ref, prev_ref, next_ref, it, tiles_pb):
    tm = main_ref.shape[0]
    zero = jnp.zeros(prev_ref.shape, F32)
    ext_scr[0:HALO, :] = jnp.where(it == 0, zero, prev_ref[...].astype(F32))
    ext_scr[HALO:HALO + tm, :] = main_ref[...].astype(F32)
    ext_scr[HALO + tm:2 * HALO + tm, :] = jnp.where(it == tiles_pb - 1, zero, next_ref[...].astype(F32))


def _pool_kernel(z_ref, zp_ref, zn_ref, wbd_ref, scale_ref, o_ref, ext_scr, d_scr, *, tiles_pb, seq):
    tm = z_ref.shape[0]
    it = pl.program_id(0) % tiles_pb
    _fill_ext(ext_scr, z_ref, zp_ref, zn_ref, it, tiles_pb)
    rc = 128

    for c in range(tm // rc):
        r0 = c * rc
        pos = it * tm + r0 + lax.broadcasted_iota(jnp.int32, (rc, 1), 0)
        for g, w in enumerate(PL_WINDOWS):
            cols = slice(g * 128, (g + 1) * 128)
            lo = r0 + HALO - w // 2
            acc = ext_scr[lo:lo + rc, cols]
            for s in range(1, w):
                acc = acc + ext_scr[lo + s:lo + s + rc, cols]
            cnt = (jnp.minimum(pos + w // 2, seq) - jnp.maximum(pos - w // 2, 0)).astype(F32)
            d = acc / cnt - ext_scr[r0 + HALO:r0 + HALO + rc, cols]
            d_scr[r0:r0 + rc, cols] = d.astype(BF16)
    o_ref[...] = (_dot(d_scr[...], wbd_ref[...]) * scale_ref[...]).astype(BF16)


def _pool(z, wbd, scale, *, tpb):
    t = z.shape[0]
    tm = min(512, tpb)
    const2 = lambda i: (0, 0)
    return pl.pallas_call(
        functools.partial(_pool_kernel, tiles_pb=tpb // tm, seq=tpb),
        grid=(t // tm,),
        in_specs=_halo_specs(tm, t, W_GRP, ZT_POOL) + [
            pl.BlockSpec((W_GRP, W_GRP), const2),
            pl.BlockSpec((1, W_GRP), const2),
        ],
        out_specs=pl.BlockSpec((tm, W_GRP), lambda i: (i, 0)),
        out_shape=jax.ShapeDtypeStruct((t, W_GRP), BF16),
        scratch_shapes=[pltpu.VMEM((tm + 2 * HALO, W_GRP), F32), pltpu.VMEM((tm, W_GRP), BF16)],
        compiler_params=_cparams(("parallel",)),
        name="pool_mixer",
    )(z, z, z, wbd, scale)


def _conv_kernel(h_ref, hp_ref, hn_ref, dww_ref, dwb_ref, ng_ref, pww_ref, pwb_ref, o_ref, ext_scr, a_scr,
                 *, tiles_pb):
    tm = h_ref.shape[0]
    it = pl.program_id(0) % tiles_pb
    _fill_ext(ext_scr, h_ref, hp_ref, hn_ref, it, tiles_pb)
    rc = 64
    base = HALO - (CV_KERNEL - 1) // 2

    for c in range(tm // rc):
        r0 = c * rc
        acc = jnp.broadcast_to(dwb_ref[...], (rc, W_GRP))
        for k in range(CV_KERNEL):
            acc = acc + dww_ref[k:k + 1, :] * ext_scr[r0 + base + k:r0 + base + k + rc, :]
        ms = jnp.mean(acc * acc, axis=-1, keepdims=True)
        hh = acc * lax.rsqrt(ms + EPS) * ng_ref[...]
        a_scr[r0:r0 + rc, :] = (hh * jax.nn.sigmoid(hh)).astype(BF16)
    o_ref[...] = (_dot(a_scr[...], pww_ref[...]) + pwb_ref[...]).astype(BF16)


def _conv(z, dww, dwb, ng, pww, pwb, *, tpb):
    t = z.shape[0]
    tm = min(512, tpb)
    const2 = lambda i: (0, 0)
    return pl.pallas_call(
        functools.partial(_conv_kernel, tiles_pb=tpb // tm),
        grid=(t // tm,),
        in_specs=_halo_specs(tm, t, W_GRP, ZT_CONV) + [
            pl.BlockSpec((CV_KERNEL, W_GRP), const2),
            pl.BlockSpec((1, W_GRP), const2),
            pl.BlockSpec((1, W_GRP), const2),
            pl.BlockSpec((W_GRP, W_GRP), const2),
            pl.BlockSpec((1, W_GRP), const2),
        ],
        out_specs=pl.BlockSpec((tm, W_GRP), lambda i: (i, 0)),
        out_shape=jax.ShapeDtypeStruct((t, W_GRP), BF16),
        scratch_shapes=[pltpu.VMEM((tm + 2 * HALO, W_GRP), F32), pltpu.VMEM((tm, W_GRP), BF16)],
        compiler_params=_cparams(("parallel",)),
        name="conformer_conv",
    )(z, z, z, dww, dwb, ng, pww, pwb)


def _outproj_kernel(yna_ref, ysg_ref, ypl_ref, ycv_ref, w_ref, x_ref, g1_ref, sh2_ref, sc2_ref, x1_ref, hn_ref):
    tm, d = x_ref.shape
    y = jnp.concatenate([yna_ref[...], ysg_ref[...], ypl_ref[...], ycv_ref[...]], axis=1)
    nc = 512
    ssq = jnp.zeros((tm, 1), F32)
    for n in range(d // nc):
        cs = slice(n * nc, (n + 1) * nc)
        x1 = x_ref[:, cs] + g1_ref[0][:, cs] * _dot(y, w_ref[:, cs])
        x1_ref[:, cs] = x1
        ssq = ssq + jnp.sum(x1 * x1, axis=-1, keepdims=True)
    r = lax.rsqrt(ssq * (1.0 / d) + EPS)
    for n in range(d // nc):
        cs = slice(n * nc, (n + 1) * nc)
        hn_ref[:, cs] = (x1_ref[:, cs] * r * (1.0 + sc2_ref[0][:, cs]) + sh2_ref[0][:, cs]).astype(BF16)


def _outproj(ys, w_out, x2d, g1, sh2, sc2, *, tpb):
    t, d = x2d.shape
    tm = min(512, tpb)
    tiles_pb = tpb // tm
    row = lambda i: (i // tiles_pb, 0, 0)
    return pl.pallas_call(
        _outproj_kernel,
        grid=(t // tm,),
        in_specs=[pl.BlockSpec((tm, W_GRP), lambda i: (i, 0))] * 4 + [
            pl.BlockSpec((d, d), lambda i: (0, 0)),
            pl.BlockSpec((tm, d), lambda i: (i, 0)),
            pl.BlockSpec((1, 1, d), row),
            pl.BlockSpec((1, 1, d), row),
            pl.BlockSpec((1, 1, d), row),
        ],
        out_specs=[pl.BlockSpec((tm, d), lambda i: (i, 0)), pl.BlockSpec((tm, d), lambda i: (i, 0))],
        out_shape=[jax.ShapeDtypeStruct((t, d), F32), jax.ShapeDtypeStruct((t, d), BF16)],
        compiler_params=_cparams(("parallel",)),
        name="out_proj",
    )(*ys, w_out, x2d, g1, sh2, sc2)


def _ffn_kernel(hn_ref, hp_ref, hx_ref, x1_ref, g2_ref, wg_ref, wv_ref, wd_ref, dw_ref, o_ref, ext_scr, g_scr,
                *, tiles_pb, nj):
    tm, d = hn_ref.shape
    j = pl.program_id(1)
    it = pl.program_id(0) % tiles_pb

    @pl.when(j == 0)
    def _():
        zero = jnp.zeros((HALO, d), BF16)
        ext_scr[0:HALO, :] = jnp.where(it == 0, zero, hp_ref[...])
        ext_scr[HALO:HALO + tm, :] = hn_ref[...]
        ext_scr[HALO + tm:2 * HALO + tm, :] = jnp.where(it == tiles_pb - 1, zero, hx_ref[...])
        o_ref[...] = jnp.zeros(o_ref.shape, F32)

    g_scr[...] = _dot(ext_scr[...], wg_ref[...])
    val = _dot(hn_ref[...], wv_ref[...])
    c = (dw_ref[0:1, :] * g_scr[HALO - 1:HALO - 1 + tm, :] + dw_ref[1:2, :] * g_scr[HALO:HALO + tm, :]
         + dw_ref[2:3, :] * g_scr[HALO + 1:HALO + 1 + tm, :])
    a = (c * jax.nn.sigmoid(c) * val).astype(BF16)
    nc = 512
    for n in range(d // nc):
        cs = slice(n * nc, (n + 1) * nc)
        o_ref[:, cs] += _dot(a, wd_ref[:, cs])

    @pl.when(j == nj - 1)
    def _():
        o_ref[...] = x1_ref[...] + g2_ref[0] * o_ref[...]


def _ffn(hn, x1, g2, wg, wv, wd, dw, *, tpb):
    t, d = x1.shape
    ff = wg.shape[1]
    tm = min(1024, tpb)
    tiles_pb = tpb // tm
    tf = 512
    nj = ff // tf
    hb = tm // HALO
    last = t // HALO - 1
    once = pl.Buffered(1)
    return pl.pallas_call(
        functools.partial(_ffn_kernel, tiles_pb=tiles_pb, nj=nj),
        grid=(t // tm, nj),
        in_specs=[
            pl.BlockSpec((tm, d), lambda i, j: (i, 0), pipeline_mode=once),
            pl.BlockSpec((HALO, d), lambda i, j: (jnp.maximum(i * hb - 1, 0), 0)),
            pl.BlockSpec((HALO, d), lambda i, j: (jnp.minimum((i + 1) * hb, last), 0)),
            pl.BlockSpec((tm, d), lambda i, j: (i, 0), pipeline_mode=once),
            pl.BlockSpec((1, 1, d), lambda i, j: (i // tiles_pb, 0, 0)),
            pl.BlockSpec((d, tf), lambda i, j: (0, j)),
            pl.BlockSpec((d, tf), lambda i, j: (0, j)),
            pl.BlockSpec((tf, d), lambda i, j: (j, 0)),
            pl.BlockSpec((FF_KERNEL, tf), lambda i, j: (0, j)),
        ],
        out_specs=pl.BlockSpec((tm, d), lambda i, j: (i, 0)),
        out_shape=jax.ShapeDtypeStruct((t, d), F32),
        scratch_shapes=[pltpu.VMEM((tm + 2 * HALO, d), BF16), pltpu.VMEM((tm + 2 * HALO, tf), F32)],
        compiler_params=_cparams(("parallel", "arbitrary")),
        name="conv_glu_ffn",
    )(hn, hn, hn, x1, g2, wg, wv, wd, dw)


def _rope_tables(seq):
    half = TA_DIM // 2
    freqs = ROPE_BASE ** (-np.arange(0, half, 2, dtype=np.float32) / half)
    pos = np.arange(seq)
    lane = np.arange(TA_DIM)
    p = np.where(lane[None, :] < half, (pos // GRID_W)[:, None], (pos % GRID_W)[:, None]).astype(np.float32)
    m = lane % half
    ang = jnp.asarray(p * freqs[m % (half // 2)][None, :])
    sign = jnp.asarray(np.where(m < half // 2, -1.0, 1.0).astype(np.float32))[None, :]
    cos = jnp.concatenate([jnp.cos(ang), jnp.ones((seq, 128 - TA_DIM), F32)], axis=1)
    sin = jnp.concatenate([jnp.sin(ang) * sign, jnp.zeros((seq, 128 - TA_DIM), F32)], axis=1)
    return cos, sin


def _layout_w_in(w):
    d = w.shape[0]
    g = W_GRP
    na, sg = w[:, 0:3 * g], w[:, 3 * g:5 * g]
    o = 5 * g
    tq, tk, tv = w[:, o:o + TA_DIM], w[:, o + TA_DIM:o + 2 * TA_DIM], w[:, o + 2 * TA_DIM:o + 3 * TA_DIM]
    rest = w[:, o + 3 * TA_DIM:]
    z64 = jnp.zeros((d, 128 - TA_DIM), w.dtype)
    ta = jnp.concatenate([tq, z64, tk, z64, tv, jnp.zeros((d, IN_TILE - 3 * 128 + 128 - TA_DIM), w.dtype)], axis=1)
    return jnp.concatenate([na, sg, rest, ta], axis=1).astype(BF16)


def _layer_params(l, w_in, na_q_gain, na_k_gain, na_rpb, sg_v_gain, sg_ws, sg_bs, ta_q_gain, ta_k_gain,
                  ta_w_out, pl_w, pl_scale, cv_dw_w, cv_dw_b, cv_norm_gain, cv_pw_w, cv_pw_b, w_out,
                  ff_w_up, ff_dw_w, ff_w_down):
    ones64 = jnp.ones((128 - TA_DIM,), F32)
    ws = sg_ws[l].astype(BF16)
    ff = ff_w_down.shape[1]
    p = dict(
        w_in=_layout_w_in(w_in[l]),
        gq=jnp.tile(na_q_gain[l], NA_HEADS)[None, :],
        gk=jnp.tile(na_k_gain[l], NA_HEADS)[None, :],
        gt=jnp.concatenate([ta_q_gain[l], ones64, ta_k_gain[l], ones64])[None, :],
        na_bias=_na_bias_table(na_rpb[l]),
        sg_gain=sg_v_gain[l][None, :],
        sg_wcat=jnp.concatenate([ws[0::2], ws[1::2]], axis=2),
        sg_bsx=jnp.repeat(sg_bs[l].T, W_GRP // SG_HEADS, axis=1),
        ta_w_out=jnp.concatenate([ta_w_out[l], jnp.zeros((128 - TA_DIM, W_GRP), F32)], axis=0).astype(BF16),
        pl_wbd=jax.scipy.linalg.block_diag(*[pl_w[l, g] for g in range(len(PL_WINDOWS))]).astype(BF16),
        pl_scale=pl_scale[l][None, :],
        cv_dw_w=cv_dw_w[l], cv_dw_b=cv_dw_b[l][None, :], cv_ng=cv_norm_gain[l][None, :],
        cv_pw_w=cv_pw_w[l].astype(BF16), cv_pw_b=cv_pw_b[l][None, :],
        w_out=w_out[l].astype(BF16),
        ff_wg=ff_w_up[l, :, :ff].astype(BF16), ff_wv=ff_w_up[l, :, ff:].astype(BF16),
        ff_dw=ff_dw_w[l], ff_wd=ff_w_down[l].astype(BF16),
    )
    return p


def _smat():
    idx = np.arange(IN_TILE) // HEAD_DIM
    return jnp.asarray((idx[:, None] == idx[None, :]).astype(np.float32)).astype(BF16)


def _mixers(p, z, na_o, ta_o, *, tpb):
    sg_o = _sg(z, ta_o, p["sg_gain"], p["sg_wcat"], p["sg_bsx"], p["ta_w_out"], tpb=tpb)
    pl_o = _pool(z, p["pl_wbd"], p["pl_scale"], tpb=tpb)
    cv_o = _conv(z, p["cv_dw_w"], p["cv_dw_b"], p["cv_ng"], p["cv_pw_w"], p["cv_pw_b"], tpb=tpb)
    return [na_o, sg_o, pl_o, cv_o]


def _tail(p, ys, x2d, mods, *, tpb):
    x1, hn = _outproj(ys, p["w_out"], x2d, mods[2], mods[3], mods[4], tpb=tpb)
    return _ffn(hn, x1, mods[5], p["ff_wg"], p["ff_wv"], p["ff_wd"], p["ff_dw"], tpb=tpb)


def kernel(x, c, ctx, c_ctx, w_mod, b_mod, w_in, na_q_gain, na_k_gain, na_rpb, sg_v_gain, sg_ws, sg_bs,
           ta_q_gain, ta_k_gain, ta_w_out, pl_w, pl_scale, cv_dw_w, cv_dw_b, cv_norm_gain, cv_pw_w, cv_pw_b,
           w_out, ff_w_up, ff_dw_w, ff_w_down):
    nb, seq, d = x.shape
    lc = ctx.shape[1]
    depth = w_mod.shape[0]
    mods_all = _mod_vectors(c, c_ctx, w_mod, b_mod).reshape(depth, 8, 6, d)
    smat = _smat()
    cos_x, sin_x = _rope_tables(seq)
    cos_c, sin_c = jnp.ones((lc, 128), F32), jnp.zeros((lc, 128), F32)

    x2d = x.reshape(nb * seq, d)
    xc2d = ctx.reshape(nb * lc, d)
    for l in range(depth):
        last = l == depth - 1
        p = _layer_params(l, w_in, na_q_gain, na_k_gain, na_rpb, sg_v_gain, sg_ws, sg_bs, ta_q_gain, ta_k_gain,
                          ta_w_out, pl_w, pl_scale, cv_dw_w, cv_dw_b, cv_norm_gain, cv_pw_w, cv_pw_b, w_out,
                          ff_w_up, ff_dw_w, ff_w_down)
        mx = [mods_all[l, :nb, k][:, None, :] for k in range(6)]
        mc = [jnp.broadcast_to(mods_all[l, nb, k][None, None, :], (nb, 1, d)) for k in range(6)]
        zc = _inproj(xc2d, mc[0], mc[1], p["w_in"], smat, p["gq"], p["gk"], p["gt"], cos_c, sin_c, tpb=lc)
        z = _inproj(x2d, mx[0], mx[1], p["w_in"], smat, p["gq"], p["gk"], p["gt"], cos_x, sin_x, tpb=seq)
        na_o = _na_x(z, zc, p["na_bias"], nb=nb, seq=seq, lc=lc)
        ta_o = _ta(z, zc, nb=nb, lq=seq, le=lc)
        ys = _mixers(p, z, na_o, ta_o, tpb=seq)
        if not last:
            na_oc = _na_ctx(zc, nb=nb, lc=lc)
            ta_oc = _ta(zc, None, nb=nb, lq=lc, le=0)
            ysc = _mixers(p, zc, na_oc, ta_oc, tpb=lc)
            xc2d = _tail(p, ysc, xc2d, mc, tpb=lc)
        x2d = _tail(p, ys, x2d, mx, tpb=seq)
    return x2d.reshape(nb, seq, d)
```

```python
import functools

import numpy as np
import jax
import jax.numpy as jnp
from jax import lax
from jax.experimental import pallas as pl
from jax.experimental.pallas import tpu as pltpu

F32 = jnp.float32
BF16 = jnp.bfloat16

D_MODEL = 2048
GRID_W = 64
HEAD_DIM = 64
W_GRP = D_MODEL // 4
NA_HEADS = W_GRP // HEAD_DIM
NA_WIN_R = 8
NA_WIN_C = 16
NA_QROWS = 4
NA_BAND = NA_WIN_R + NA_QROWS
NA_SUB = 4
LOG2E = 1.4426950408889634
SG_HEADS = 8
SG_CHUNK = 128
TA_DIM = 64
PL_WINDOWS = (2, 4, 8, 16)
CV_KERNEL = 31
FF_KERNEL = 3
ROPE_BASE = 10000.0
EPS = 1e-6
HALO = 16
SUBLANES = 8
IN_TILE = 512
N_IN_TILES = 9
Z_COLS = 8 * IN_TILE
ZB_NA_Q, ZB_NA_K, ZB_NA_V = 0, 4, 8
ZB_TA_Q, ZB_TA_K, ZB_TA_V = 28, 29, 30
ZT_SG_U, ZT_SG_V, ZT_POOL, ZT_CONV = 3, 4, 5, 6
NEG_BIG = -1e30
NT_DIMS = (((1,), (1,)), ((), ()))
VMEM_LIMIT_MB = 56


def _cparams(sem, vmem_mb=VMEM_LIMIT_MB):
    return pltpu.CompilerParams(dimension_semantics=sem, vmem_limit_bytes=vmem_mb * 1024 * 1024)


def _dot(a, b):
    return jnp.dot(a, b, preferred_element_type=F32)


def _dot_nt(a, b):
    return lax.dot_general(a, b, NT_DIMS, preferred_element_type=F32)


def _mod_kernel(c_ref, w_ref, b_ref, o_ref):
    cv = c_ref[...]
    s = cv * jax.nn.sigmoid(cv)
    o_ref[0] = _dot(s.astype(BF16), w_ref[0].astype(BF16)) + b_ref[0]


def _mod_vectors(c, c_ctx, w_mod, b_mod):
    depth, d, n = w_mod.shape
    nb = c.shape[0]
    assert nb + 1 <= 8
    cv = jnp.zeros((8, d), F32).at[:nb].set(c).at[nb].set(c_ctx)
    tn = 1024
    return pl.pallas_call(
        _mod_kernel,
        grid=(depth, n // tn),
        in_specs=[
            pl.BlockSpec((8, d), lambda l, k: (0, 0)),
            pl.BlockSpec((1, d, tn), lambda l, k: (l, 0, k)),
            pl.BlockSpec((1, 1, tn), lambda l, k: (l, 0, k)),
        ],
        out_specs=pl.BlockSpec((1, 8, tn), lambda l, k: (l, 0, k)),
        out_shape=jax.ShapeDtypeStruct((depth, 8, n), F32),
        compiler_params=_cparams(("parallel", "parallel"), 40),
        name="mod_vectors",
    )(cv, w_mod, b_mod.reshape(depth, 1, n))


def _rope(t, cos, sin):
    lane = lax.broadcasted_iota(jnp.int32, t.shape, 1)
    partner = jnp.where((lane % 32) < 16, pltpu.roll(t, 128 - 16, 1), pltpu.roll(t, 16, 1))
    return t * cos + partner * sin


def _inproj_kernel(x_ref, sh_ref, sc_ref, w_ref, smat_ref, gq_ref, gk_ref, gt_ref, cos_ref, sin_ref,
                   z_ref, xn_scr, a_scr):
    j = pl.program_id(1)
    tm = x_ref.shape[0]
    parts = 2
    pr = tm // parts

    def normalize(rs):
        x = x_ref[rs, :]
        ms = jnp.mean(x * x, axis=-1, keepdims=True)
        xn = x * lax.rsqrt(ms + EPS) * (1.0 + sc_ref[0]) + sh_ref[0]
        xn_scr[rs, :] = xn.astype(BF16)

    def head_rms(a, gain):
        sq = a * a
        hi = sq.astype(BF16)
        lo = (sq - hi.astype(F32)).astype(BF16)
        ss = _dot(hi, smat_ref[...]) + _dot(lo, smat_ref[...])
        return a * lax.rsqrt(ss * (1.0 / HEAD_DIM) + EPS) * gain

    def epi_q(rs, acc):
        z_ref[rs, :] = (head_rms(acc, gq_ref[...]) * (HEAD_DIM ** -0.5 * LOG2E)).astype(BF16)

    def epi_k(rs, acc):
        z_ref[rs, :] = head_rms(acc, gk_ref[...]).astype(BF16)

    def epi_id(rs, acc):
        z_ref[rs, :] = acc.astype(BF16)

    def epi_gelu(rs, acc):
        z_ref[rs, :] = jax.nn.gelu(acc).astype(BF16)

    def epi_hold(rs, acc):
        a_scr[rs, :] = acc

    def epi_glu(rs, acc):
        z_ref[rs, :] = (a_scr[rs, :] * jax.nn.sigmoid(acc)).astype(BF16)

    def epi_ta(rs, acc):
        def nrm(t, g):
            ms = jnp.sum(t * t, axis=-1, keepdims=True) * (1.0 / TA_DIM)
            return t * lax.rsqrt(ms + EPS) * g
        cos = cos_ref[rs, :]
        sin = sin_ref[rs, :]
        tq = _rope(nrm(acc[:, 0:128], gt_ref[:, 0:128]), cos, sin) * (TA_DIM ** -0.5 * LOG2E)
        tk = _rope(nrm(acc[:, 128:256], gt_ref[:, 128:256]), cos, sin)
        tv = acc[:, 256:384]
        lane = lax.broadcasted_iota(jnp.int32, tv.shape, 1)
        z_ref[rs, 0:128] = tq.astype(BF16)
        z_ref[rs, 128:256] = tk.astype(BF16)
        z_ref[rs, 256:384] = jnp.where(lane == TA_DIM, 1.0, tv).astype(BF16)
        z_ref[rs, 384:512] = acc[:, 384:512].astype(BF16)

    def run(epilogue, first=False):
        def body():
            for r in range(parts):
                rs = slice(r * pr, (r + 1) * pr)
                if first:
                    normalize(rs)
                epilogue(rs, _dot(xn_scr[rs, :], w_ref[...]))
        return body

    pl.when(j == 0)(run(epi_q, first=True))
    pl.when(j == 1)(run(epi_k))
    pl.when(jnp.logical_or(j == 2, j == 5))(run(epi_id))
    pl.when(jnp.logical_or(j == 3, j == 4))(run(epi_gelu))
    pl.when(j == 6)(run(epi_hold))
    pl.when(j == 7)(run(epi_glu))
    pl.when(j == 8)(run(epi_ta))


def _inproj(x2d, sh, sc, w_r, smat, gq, gk, gt, cos_t, sin_t, *, tpb):
    t, d = x2d.shape
    tm = min(1024, tpb)
    tiles_pb = tpb // tm
    const = lambda i, j: (0, 0)
    return pl.pallas_call(
        _inproj_kernel,
        grid=(t // tm, N_IN_TILES),
        in_specs=[
            pl.BlockSpec((tm, d), lambda i, j: (i, 0)),
            pl.BlockSpec((1, 1, d), lambda i, j: (i // tiles_pb, 0, 0)),
            pl.BlockSpec((1, 1, d), lambda i, j: (i // tiles_pb, 0, 0)),
            pl.BlockSpec((d, IN_TILE), lambda i, j: (0, j)),
            pl.BlockSpec((IN_TILE, IN_TILE), const),
            pl.BlockSpec((1, IN_TILE), const),
            pl.BlockSpec((1, IN_TILE), const),
            pl.BlockSpec((1, 256), const),
            pl.BlockSpec((tm, 128), lambda i, j: (i % tiles_pb, 0)),
            pl.BlockSpec((tm, 128), lambda i, j: (i % tiles_pb, 0)),
        ],
        out_specs=pl.BlockSpec((tm, IN_TILE), lambda i, j: (i, jnp.where(j >= 7, j - 1, j))),
        out_shape=jax.ShapeDtypeStruct((t, Z_COLS), BF16),
        scratch_shapes=[pltpu.VMEM((tm, d), BF16), pltpu.VMEM((tm, IN_TILE), F32)],
        compiler_params=_cparams(("parallel", "arbitrary")),
        name="in_proj",
    )(x2d, sh, sc, w_r, smat, gq, gk, gt, cos_t, sin_t)


def _stack_heads(q):
    lane = lax.broadcasted_iota(jnp.int32, q.shape, 1)
    zero = jnp.zeros_like(q)
    return jnp.concatenate([jnp.where(lane < HEAD_DIM, q, zero), jnp.where(lane >= HEAD_DIM, q, zero)], axis=0)


def _unstack_heads(o2):
    m = o2.shape[0] // 2
    lane = lax.broadcasted_iota(jnp.int32, (m, 128), 1)
    return jnp.where(lane < HEAD_DIM, o2[:m], o2[m:])


def _na_x_kernel(q_ref, k_ref, v_ref, kc_ref, vc_ref, b_ref, o_ref, *, rows):
    tq = NA_QROWS * GRID_W
    nkeys = NA_BAND * GRID_W
    nrb = rows // NA_QROWS
    kc = kc_ref[...]
    vc = vc_ref[...]
    for sub in range(NA_SUB):
        rb = pl.program_id(2) * NA_SUB + sub
        rs = jnp.clip(rb * NA_QROWS - NA_WIN_R // 2, 0, rows - NA_BAND)
        start = pl.multiple_of(rs * GRID_W, GRID_W)
        case = jnp.where(rb == 0, 0, jnp.where(rb == nrb - 1, 2, 1))
        kw = k_ref[pl.ds(start, nkeys), :]
        vw = v_ref[pl.ds(start, nkeys), :]
        q2 = _stack_heads(q_ref[sub * tq:(sub + 1) * tq, :])
        s1 = _dot_nt(q2, kw) + b_ref[case, 0]
        s2 = _dot_nt(q2, kc)
        m = jnp.maximum(jnp.max(s1, axis=-1, keepdims=True), jnp.max(s2, axis=-1, keepdims=True))
        p1 = jnp.exp2(s1 - m)
        p2 = jnp.exp2(s2 - m)
        l = jnp.sum(p1, axis=-1, keepdims=True) + jnp.sum(p2, axis=-1, keepdims=True)
        o2 = _dot(p1.astype(BF16), vw) + _dot(p2.astype(BF16), vc)
        o_ref[sub * tq:(sub + 1) * tq, :] = _unstack_heads(o2 / l).astype(BF16)


def _na_x(z, zc, bias, *, nb, seq, lc):
    rows = seq // GRID_W
    assert rows % (NA_QROWS * NA_SUB) == 0 and rows >= NA_BAND
    nsteps = rows // (NA_QROWS * NA_SUB)
    tq = NA_QROWS * GRID_W
    return pl.pallas_call(
        functools.partial(_na_x_kernel, rows=rows),
        grid=(nb, NA_HEADS // 2, nsteps),
        in_specs=[
            pl.BlockSpec((NA_SUB * tq, 128), lambda b, hp, st: (b * nsteps + st, ZB_NA_Q + hp)),
            pl.BlockSpec((seq, 128), lambda b, hp, st: (b, ZB_NA_K + hp)),
            pl.BlockSpec((seq, 128), lambda b, hp, st: (b, ZB_NA_V + hp)),
            pl.BlockSpec((lc, 128), lambda b, hp, st: (b, ZB_NA_K + hp)),
            pl.BlockSpec((lc, 128), lambda b, hp, st: (b, ZB_NA_V + hp)),
            pl.BlockSpec((3, 1, 2 * tq, NA_BAND * GRID_W), lambda b, hp, st: (0, hp, 0, 0)),
        ],
        out_specs=pl.BlockSpec((NA_SUB * tq, 128), lambda b, hp, st: (b * nsteps + st, hp)),
        out_shape=jax.ShapeDtypeStruct((nb * seq, W_GRP), BF16),
        compiler_params=_cparams(("parallel", "parallel", "arbitrary")),
        name="na_attn",
    )(z, z, z, zc, zc, bias)


def _na_bias_table(rpb):
    h = rpb.shape[0]
    nq = NA_QROWS * GRID_W
    nk = NA_BAND * GRID_W
    qc, kc = np.arange(GRID_W), np.arange(GRID_W)
    c0 = np.clip(qc - NA_WIN_C // 2, 0, GRID_W - NA_WIN_C)
    col_ok = (kc[None, :] >= c0[:, None]) & (kc[None, :] < c0[:, None] + NA_WIN_C)
    cidx = np.clip(kc[None, :] - qc[:, None] + NA_WIN_C - 1, 0, 2 * NA_WIN_C - 2)
    col_sel = jnp.asarray((cidx.reshape(-1)[None, :] == np.arange(2 * NA_WIN_C - 1)[:, None]).astype(np.float32))
    qr, wr = np.arange(NA_QROWS), np.arange(NA_BAND)
    half = NA_WIN_R // 2
    cases = ((0, np.zeros(NA_QROWS, int)), (half, np.arange(NA_QROWS)), (NA_WIN_R, np.full(NA_QROWS, half)))
    tabs = []
    for dq, r0 in cases:
        row_ok = (wr[None, :] >= r0[:, None]) & (wr[None, :] < r0[:, None] + NA_WIN_R)
        ridx = np.clip(wr[None, :] - (dq + qr[:, None]) + NA_WIN_R - 1, 0, 2 * NA_WIN_R - 2)
        rows = rpb[:, ridx.reshape(-1), :]
        blocks = jnp.einsum("hrc,cx->hrx", rows, col_sel, precision=lax.Precision.HIGHEST)
        blocks = blocks.reshape(h, NA_QROWS, NA_BAND, GRID_W, GRID_W).transpose(0, 1, 3, 2, 4)
        valid = row_ok[:, None, :, None] & col_ok[None, :, None, :]
        tabs.append(jnp.where(jnp.asarray(valid)[None], blocks * LOG2E, NEG_BIG).reshape(h, nq, nk))
    tab = jnp.stack(tabs)
    return tab.reshape(3, h // 2, 2 * nq, nk).astype(F32)


def _na_ctx_kernel(q_ref, k_ref, v_ref, o_ref):
    q2 = _stack_heads(q_ref[...])
    s = _dot_nt(q2, k_ref[...])
    m = jnp.max(s, axis=-1, keepdims=True)
    p = jnp.exp2(s - m)
    l = jnp.sum(p, axis=-1, keepdims=True)
    o2 = _dot(p.astype(BF16), v_ref[...])
    o_ref[...] = _unstack_heads(o2 / l).astype(BF16)


def _na_ctx(zc, *, nb, lc):
    return pl.pallas_call(
        _na_ctx_kernel,
        grid=(nb, NA_HEADS // 2),
        in_specs=[
            pl.BlockSpec((lc, 128), lambda b, hp: (b, ZB_NA_Q + hp)),
            pl.BlockSpec((lc, 128), lambda b, hp: (b, ZB_NA_K + hp)),
            pl.BlockSpec((lc, 128), lambda b, hp: (b, ZB_NA_V + hp)),
        ],
        out_specs=pl.BlockSpec((lc, 128), lambda b, hp: (b, hp)),
        out_shape=jax.ShapeDtypeStruct((nb * lc, W_GRP), BF16),
        compiler_params=_cparams(("parallel", "parallel")),
        name="na_attn_ctx",
    )(zc, zc, zc)


def _ta_kernel(*refs, tk, nk, has_extra):
    if has_extra:
        q_ref, k_ref, v_ref, ke_ref, ve_ref, o_ref = refs
    else:
        q_ref, k_ref, v_ref, o_ref = refs
    q = q_ref[...]
    tq = q.shape[0]

    def step(kb, vb, carry):
        m, acc = carry
        s = _dot_nt(q, kb)
        mn = jnp.maximum(m, jnp.max(s, axis=-1, keepdims=True))
        p = jnp.exp2(s - mn).astype(BF16)
        acc = jnp.exp2(m - mn) * acc + _dot(p, vb)
        return mn, acc

    def body(c, carry):
        st = pl.multiple_of(c * tk, tk)
        return step(k_ref[pl.ds(st, tk), :], v_ref[pl.ds(st, tk), :], carry)

    carry = (jnp.full((tq, 1), NEG_BIG, F32), jnp.zeros((tq, 128), F32))
    carry = lax.fori_loop(0, nk, body, carry, unroll=4 if nk % 4 == 0 else 1)
    if has_extra:
        carry = step(ke_ref[...], ve_ref[...], carry)
    _, acc = carry
    lane = lax.broadcasted_iota(jnp.int32, acc.shape, 1)
    l = jnp.sum(jnp.where(lane == TA_DIM, acc, 0.0), axis=-1, keepdims=True)
    o_ref[...] = jnp.where(lane < TA_DIM, acc / l, 0.0).astype(BF16)


def _ta(zq, zextra, *, nb, lq, le):
    tq = min(512, lq)
    tk = min(1024, lq)
    nqb = lq // tq
    has_extra = zextra is not None
    in_specs = [
        pl.BlockSpec((tq, 128), lambda b, qi: (b * nqb + qi, ZB_TA_Q)),
        pl.BlockSpec((lq, 128), lambda b, qi: (b, ZB_TA_K)),
        pl.BlockSpec((lq, 128), lambda b, qi: (b, ZB_TA_V)),
    ]
    args = [zq, zq, zq]
    if has_extra:
        in_specs += [
            pl.BlockSpec((le, 128), lambda b, qi: (b, ZB_TA_K)),
            pl.BlockSpec((le, 128), lambda b, qi: (b, ZB_TA_V)),
        ]
        args += [zextra, zextra]
    return pl.pallas_call(
        functools.partial(_ta_kernel, tk=tk, nk=lq // tk, has_extra=has_extra),
        grid=(nb, nqb),
        in_specs=in_specs,
        out_specs=pl.BlockSpec((tq, 128), lambda b, qi: (b * nqb + qi, 0)),
        out_shape=jax.ShapeDtypeStruct((nb * lq, 128), BF16),
        compiler_params=_cparams(("parallel", "arbitrary")),
        name="ta_attn",
    )(*args)


def _sg_kernel(u_ref, gv_ref, tao_ref, gain_ref, wcat_ref, bsx_ref, two_ref, o_ref, v_scr):
    tm = u_ref.shape[0]
    gv = gv_ref[...].astype(F32)
    ms = jnp.mean(gv * gv, axis=-1, keepdims=True)
    v_scr[...] = (gv * lax.rsqrt(ms + EPS) * gain_ref[...]).astype(BF16)
    lane = lax.broadcasted_iota(jnp.int32, (SG_CHUNK, 128), 1)
    zero = jnp.zeros((SG_CHUNK, 128), BF16)

    def chunk(c, carry):
        r0 = pl.multiple_of(c * SG_CHUNK, SG_CHUNK)
        rows = pl.ds(r0, SG_CHUNK)
        outs = []
        for p in range(SG_HEADS // 2):
            vs = v_scr[rows, p * 128:(p + 1) * 128]
            rhs = jnp.concatenate([jnp.where(lane < 64, vs, zero), jnp.where(lane >= 64, vs, zero)], axis=0)
            outs.append(_dot(wcat_ref[p], rhs))
        s = jnp.concatenate(outs, axis=1) + bsx_ref[...] + _dot(tao_ref[rows, :], two_ref[...])
        o_ref[rows, :] = (u_ref[rows, :].astype(F32) * s).astype(BF16)
        return carry

    lax.fori_loop(0, tm // SG_CHUNK, chunk, 0)


def _sg(z, tao, gain, wcat, bsx, two, *, tpb):
    t = z.shape[0]
    tm = min(512, tpb)
    const2 = lambda i: (0, 0)
    return pl.pallas_call(
        _sg_kernel,
        grid=(t // tm,),
        in_specs=[
            pl.BlockSpec((tm, W_GRP), lambda i: (i, ZT_SG_U)),
            pl.BlockSpec((tm, W_GRP), lambda i: (i, ZT_SG_V)),
            pl.BlockSpec((tm, 128), lambda i: (i, 0)),
            pl.BlockSpec((1, W_GRP), const2),
            pl.BlockSpec((SG_HEADS // 2, SG_CHUNK, 2 * SG_CHUNK), lambda i: (0, 0, 0)),
            pl.BlockSpec((SG_CHUNK, W_GRP), const2),
            pl.BlockSpec((128, W_GRP), const2),
        ],
        out_specs=pl.BlockSpec((tm, W_GRP), lambda i: (i, 0)),
        out_shape=jax.ShapeDtypeStruct((t, W_GRP), BF16),
        scratch_shapes=[pltpu.VMEM((tm, W_GRP), BF16)],
        compiler_params=_cparams(("parallel",)),
        name="sg_unit",
    )(z, z, tao, gain, wcat, bsx, two)


def _halo_specs(tm, t, width, col):
    hb = tm // HALO
    last = t // HALO - 1
    return [
        pl.BlockSpec((tm, width), lambda i: (i, col)),
        pl.BlockSpec((HALO, width), lambda i: (jnp.maximum(i * hb - 1, 0), col)),
        pl.BlockSpec((HALO, width), lambda i: (jnp.minimum((i + 1) * hb, last), col)),
    ]


def _fill_ext(ext_scr, main_ref, prev_ref, next_ref, it, tiles_pb):
    tm = main_ref.shape[0]
    zero = jnp.zeros(prev_ref.shape, F32)
    ext_scr[0:HALO, :] = jnp.where(it == 0, zero, prev_ref[...].astype(F32))
    ext_scr[HALO:HALO + tm, :] = main_ref[...].astype(F32)
    ext_scr[HALO + tm:2 * HALO + tm, :] = jnp.where(it == tiles_pb - 1, zero, next_ref[...].astype(F32))


def _pool_kernel(z_ref, zp_ref, zn_ref, wbd_ref, scale_ref, o_ref, ext_scr, d_scr, *, tiles_pb, seq):
    tm = z_ref.shape[0]
    it = pl.program_id(0) % tiles_pb
    _fill_ext(ext_scr, z_ref, zp_ref, zn_ref, it, tiles_pb)
    rc = 128

    for c in range(tm // rc):
        r0 = c * rc
        pos = it * tm + r0 + lax.broadcasted_iota(jnp.int32, (rc, 1), 0)
        for g, w in enumerate(PL_WINDOWS):
            cols = slice(g * 128, (g + 1) * 128)
            lo = r0 + HALO - w // 2
            acc = ext_scr[lo:lo + rc, cols]
            for s in range(1, w):
                acc = acc + ext_scr[lo + s:lo + s + rc, cols]
            cnt = (jnp.minimum(pos + w // 2, seq) - jnp.maximum(pos - w // 2, 0)).astype(F32)
            d = acc / cnt - ext_scr[r0 + HALO:r0 + HALO + rc, cols]
            d_scr[r0:r0 + rc, cols] = d.astype(BF16)
    o_ref[...] = (_dot(d_scr[...], wbd_ref[...]) * scale_ref[...]).astype(BF16)


def _pool(z, wbd, scale, *, tpb):
    t = z.shape[0]
    tm = min(512, tpb)
    const2 = lambda i: (0, 0)
    return pl.pallas_call(
        functools.partial(_pool_kernel, tiles_pb=tpb // tm, seq=tpb),
        grid=(t // tm,),
        in_specs=_halo_specs(tm, t, W_GRP, ZT_POOL) + [
            pl.BlockSpec((W_GRP, W_GRP), const2),
            pl.BlockSpec((1, W_GRP), const2),
        ],
        out_specs=pl.BlockSpec((tm, W_GRP), lambda i: (i, 0)),
        out_shape=jax.ShapeDtypeStruct((t, W_GRP), BF16),
        scratch_shapes=[pltpu.VMEM((tm + 2 * HALO, W_GRP), F32), pltpu.VMEM((tm, W_GRP), BF16)],
        compiler_params=_cparams(("parallel",)),
        name="pool_mixer",
    )(z, z, z, wbd, scale)


def _conv_kernel(h_ref, hp_ref, hn_ref, dww_ref, dwb_ref, ng_ref, pww_ref, pwb_ref, o_ref, ext_scr, sh_scr,
                 a_scr, *, tiles_pb):
    tm = h_ref.shape[0]
    it = pl.program_id(0) % tiles_pb
    _fill_ext(ext_scr, h_ref, hp_ref, hn_ref, it, tiles_pb)
    rc = 64
    base = HALO - (CV_KERNEL - 1) // 2
    nsh = sh_scr.shape[1]
    for s in range(1, SUBLANES):
        sh_scr[s - 1] = ext_scr[s:s + nsh, :]

    for c in range(tm // rc):
        r0 = c * rc
        acc = jnp.broadcast_to(dwb_ref[...], (rc, W_GRP))
        for k in range(CV_KERNEL):
            a, s = divmod(base + k, SUBLANES)
            lo = r0 + a * SUBLANES
            rows = ext_scr[lo:lo + rc, :] if s == 0 else sh_scr[s - 1, lo:lo + rc, :]
            acc = acc + dww_ref[k:k + 1, :] * rows
        ms = jnp.mean(acc * acc, axis=-1, keepdims=True)
        hh = acc * lax.rsqrt(ms + EPS) * ng_ref[...]
        a_scr[r0:r0 + rc, :] = (hh * jax.nn.sigmoid(hh)).astype(BF16)
    o_ref[...] = (_dot(a_scr[...], pww_ref[...]) + pwb_ref[...]).astype(BF16)


def _conv(z, dww, dwb, ng, pww, pwb, *, tpb):
    t = z.shape[0]
    tm = min(512, tpb)
    const2 = lambda i: (0, 0)
    return pl.pallas_call(
        functools.partial(_conv_kernel, tiles_pb=tpb // tm),
        grid=(t // tm,),
        in_specs=_halo_specs(tm, t, W_GRP, ZT_CONV) + [
            pl.BlockSpec((CV_KERNEL, W_GRP), const2),
            pl.BlockSpec((1, W_GRP), const2),
            pl.BlockSpec((1, W_GRP), const2),
            pl.BlockSpec((W_GRP, W_GRP), const2),
            pl.BlockSpec((1, W_GRP), const2),
        ],
        out_specs=pl.BlockSpec((tm, W_GRP), lambda i: (i, 0)),
        out_shape=jax.ShapeDtypeStruct((t, W_GRP), BF16),
        scratch_shapes=[pltpu.VMEM((tm + 2 * HALO, W_GRP), F32),
                        pltpu.VMEM((SUBLANES - 1, tm + 2 * HALO - SUBLANES, W_GRP), F32),
                        pltpu.VMEM((tm, W_GRP), BF16)],
        compiler_params=_cparams(("parallel",)),
        name="conformer_conv",
    )(z, z, z, dww, dwb, ng, pww, pwb)


def _outproj_kernel(yna_ref, ysg_ref, ypl_ref, ycv_ref, w_ref, x_ref, g1_ref, sh2_ref, sc2_ref, x1_ref, hn_ref):
    tm, d = x_ref.shape
    y = jnp.concatenate([yna_ref[...], ysg_ref[...], ypl_ref[...], ycv_ref[...]], axis=1)
    nc = 512
    ssq = jnp.zeros((tm, 1), F32)
    for n in range(d // nc):
        cs = slice(n * nc, (n + 1) * nc)
        x1 = x_ref[:, cs] + g1_ref[0][:, cs] * _dot(y, w_ref[:, cs])
        x1_ref[:, cs] = x1
        ssq = ssq + jnp.sum(x1 * x1, axis=-1, keepdims=True)
    r = lax.rsqrt(ssq * (1.0 / d) + EPS)
    for n in range(d // nc):
        cs = slice(n * nc, (n + 1) * nc)
        hn_ref[:, cs] = (x1_ref[:, cs] * r * (1.0 + sc2_ref[0][:, cs]) + sh2_ref[0][:, cs]).astype(BF16)


def _outproj(ys, w_out, x2d, g1, sh2, sc2, *, tpb):
    t, d = x2d.shape
    tm = min(512, tpb)
    tiles_pb = tpb // tm
    row = lambda i: (i // tiles_pb, 0, 0)
    return pl.pallas_call(
        _outproj_kernel,
        grid=(t // tm,),
        in_specs=[pl.BlockSpec((tm, W_GRP), lambda i: (i, 0))] * 4 + [
            pl.BlockSpec((d, d), lambda i: (0, 0)),
            pl.BlockSpec((tm, d), lambda i: (i, 0)),
            pl.BlockSpec((1, 1, d), row),
            pl.BlockSpec((1, 1, d), row),
            pl.BlockSpec((1, 1, d), row),
        ],
        out_specs=[pl.BlockSpec((tm, d), lambda i: (i, 0)), pl.BlockSpec((tm, d), lambda i: (i, 0))],
        out_shape=[jax.ShapeDtypeStruct((t, d), F32), jax.ShapeDtypeStruct((t, d), BF16)],
        compiler_params=_cparams(("parallel",)),
        name="out_proj",
    )(*ys, w_out, x2d, g1, sh2, sc2)


def _ffn_kernel(hn_ref, hp_ref, hx_ref, x1_ref, g2_ref, wg_ref, wv_ref, wd_ref, dw_ref, o_ref, ext_scr, g_scr,
                *, tiles_pb, nj):
    tm, d = hn_ref.shape
    j = pl.program_id(1)
    it = pl.program_id(0) % tiles_pb

    @pl.when(j == 0)
    def _():
        zero = jnp.zeros((HALO, d), BF16)
        ext_scr[0:HALO, :] = jnp.where(it == 0, zero, hp_ref[...])
        ext_scr[HALO:HALO + tm, :] = hn_ref[...]
        ext_scr[HALO + tm:2 * HALO + tm, :] = jnp.where(it == tiles_pb - 1, zero, hx_ref[...])
        o_ref[...] = jnp.zeros(o_ref.shape, F32)

    g_scr[...] = _dot(ext_scr[...], wg_ref[...])
    val = _dot(hn_ref[...], wv_ref[...])
    c = (dw_ref[0:1, :] * g_scr[HALO - 1:HALO - 1 + tm, :] + dw_ref[1:2, :] * g_scr[HALO:HALO + tm, :]
         + dw_ref[2:3, :] * g_scr[HALO + 1:HALO + 1 + tm, :])
    a = (c * jax.nn.sigmoid(c) * val).astype(BF16)
    nc = 512
    for n in range(d // nc):
        cs = slice(n * nc, (n + 1) * nc)
        o_ref[:, cs] += _dot(a, wd_ref[:, cs])

    @pl.when(j == nj - 1)
    def _():
        o_ref[...] = x1_ref[...] + g2_ref[0] * o_ref[...]


def _ffn(hn, x1, g2, wg, wv, wd, dw, *, tpb):
    t, d = x1.shape
    ff = wg.shape[1]
    tm = min(1024, tpb)
    tiles_pb = tpb // tm
    tf = 512
    nj = ff // tf
    hb = tm // HALO
    last = t // HALO - 1
    once = pl.Buffered(1)
    return pl.pallas_call(
        functools.partial(_ffn_kernel, tiles_pb=tiles_pb, nj=nj),
        grid=(t // tm, nj),
        in_specs=[
            pl.BlockSpec((tm, d), lambda i, j: (i, 0), pipeline_mode=once),
            pl.BlockSpec((HALO, d), lambda i, j: (jnp.maximum(i * hb - 1, 0), 0)),
            pl.BlockSpec((HALO, d), lambda i, j: (jnp.minimum((i + 1) * hb, last), 0)),
            pl.BlockSpec((tm, d), lambda i, j: (i, 0), pipeline_mode=once),
            pl.BlockSpec((1, 1, d), lambda i, j: (i // tiles_pb, 0, 0)),
            pl.BlockSpec((d, tf), lambda i, j: (0, j)),
            pl.BlockSpec((d, tf), lambda i, j: (0, j)),
            pl.BlockSpec((tf, d), lambda i, j: (j, 0)),
            pl.BlockSpec((FF_KERNEL, tf), lambda i, j: (0, j)),
        ],
        out_specs=pl.BlockSpec((tm, d), lambda i, j: (i, 0)),
        out_shape=jax.ShapeDtypeStruct((t, d), F32),
        scratch_shapes=[pltpu.VMEM((tm + 2 * HALO, d), BF16), pltpu.VMEM((tm + 2 * HALO, tf), F32)],
        compiler_params=_cparams(("parallel", "arbitrary")),
        name="conv_glu_ffn",
    )(hn, hn, hn, x1, g2, wg, wv, wd, dw)


def _rope_tables(seq):
    half = TA_DIM // 2
    freqs = ROPE_BASE ** (-np.arange(0, half, 2, dtype=np.float32) / half)
    pos = np.arange(seq)
    lane = np.arange(TA_DIM)
    p = np.where(lane[None, :] < half, (pos // GRID_W)[:, None], (pos % GRID_W)[:, None]).astype(np.float32)
    m = lane % half
    ang = jnp.asarray(p * freqs[m % (half // 2)][None, :])
    sign = jnp.asarray(np.where(m < half // 2, -1.0, 1.0).astype(np.float32))[None, :]
    cos = jnp.concatenate([jnp.cos(ang), jnp.ones((seq, 128 - TA_DIM), F32)], axis=1)
    sin = jnp.concatenate([jnp.sin(ang) * sign, jnp.zeros((seq, 128 - TA_DIM), F32)], axis=1)
    return cos, sin


def _layout_w_in(w):
    d = w.shape[0]
    g = W_GRP
    na, sg = w[:, 0:3 * g], w[:, 3 * g:5 * g]
    o = 5 * g
    tq, tk, tv = w[:, o:o + TA_DIM], w[:, o + TA_DIM:o + 2 * TA_DIM], w[:, o + 2 * TA_DIM:o + 3 * TA_DIM]
    rest = w[:, o + 3 * TA_DIM:]
    z64 = jnp.zeros((d, 128 - TA_DIM), w.dtype)
    ta = jnp.concatenate([tq, z64, tk, z64, tv, jnp.zeros((d, IN_TILE - 3 * 128 + 128 - TA_DIM), w.dtype)], axis=1)
    return jnp.concatenate([na, sg, rest, ta], axis=1).astype(BF16)


def _layer_params(l, w_in, na_q_gain, na_k_gain, na_rpb, sg_v_gain, sg_ws, sg_bs, ta_q_gain, ta_k_gain,
                  ta_w_out, pl_w, pl_scale, cv_dw_w, cv_dw_b, cv_norm_gain, cv_pw_w, cv_pw_b, w_out,
                  ff_w_up, ff_dw_w, ff_w_down):
    ones64 = jnp.ones((128 - TA_DIM,), F32)
    ws = sg_ws[l].astype(BF16)
    ff = ff_w_down.shape[1]
    p = dict(
        w_in=_layout_w_in(w_in[l]),
        gq=jnp.tile(na_q_gain[l], NA_HEADS)[None, :],
        gk=jnp.tile(na_k_gain[l], NA_HEADS)[None, :],
        gt=jnp.concatenate([ta_q_gain[l], ones64, ta_k_gain[l], ones64])[None, :],
        na_bias=_na_bias_table(na_rpb[l]),
        sg_gain=sg_v_gain[l][None, :],
        sg_wcat=jnp.concatenate([ws[0::2], ws[1::2]], axis=2),
        sg_bsx=jnp.repeat(sg_bs[l].T, W_GRP // SG_HEADS, axis=1),
        ta_w_out=jnp.concatenate([ta_w_out[l], jnp.zeros((128 - TA_DIM, W_GRP), F32)], axis=0).astype(BF16),
        pl_wbd=jax.scipy.linalg.block_diag(*[pl_w[l, g] for g in range(len(PL_WINDOWS))]).astype(BF16),
        pl_scale=pl_scale[l][None, :],
        cv_dw_w=cv_dw_w[l], cv_dw_b=cv_dw_b[l][None, :], cv_ng=cv_norm_gain[l][None, :],
        cv_pw_w=cv_pw_w[l].astype(BF16), cv_pw_b=cv_pw_b[l][None, :],
        w_out=w_out[l].astype(BF16),
        ff_wg=ff_w_up[l, :, :ff].astype(BF16), ff_wv=ff_w_up[l, :, ff:].astype(BF16),
        ff_dw=ff_dw_w[l], ff_wd=ff_w_down[l].astype(BF16),
    )
    return p


def _smat():
    idx = np.arange(IN_TILE) // HEAD_DIM
    return jnp.asarray((idx[:, None] == idx[None, :]).astype(np.float32)).astype(BF16)


def _mixers(p, z, na_o, ta_o, *, tpb):
    sg_o = _sg(z, ta_o, p["sg_gain"], p["sg_wcat"], p["sg_bsx"], p["ta_w_out"], tpb=tpb)
    pl_o = _pool(z, p["pl_wbd"], p["pl_scale"], tpb=tpb)
    cv_o = _conv(z, p["cv_dw_w"], p["cv_dw_b"], p["cv_ng"], p["cv_pw_w"], p["cv_pw_b"], tpb=tpb)
    return [na_o, sg_o, pl_o, cv_o]


def _tail(p, ys, x2d, mods, *, tpb):
    x1, hn = _outproj(ys, p["w_out"], x2d, mods[2], mods[3], mods[4], tpb=tpb)
    return _ffn(hn, x1, mods[5], p["ff_wg"], p["ff_wv"], p["ff_wd"], p["ff_dw"], tpb=tpb)


def kernel(x, c, ctx, c_ctx, w_mod, b_mod, w_in, na_q_gain, na_k_gain, na_rpb, sg_v_gain, sg_ws, sg_bs,
           ta_q_gain, ta_k_gain, ta_w_out, pl_w, pl_scale, cv_dw_w, cv_dw_b, cv_norm_gain, cv_pw_w, cv_pw_b,
           w_out, ff_w_up, ff_dw_w, ff_w_down):
    nb, seq, d = x.shape
    lc = ctx.shape[1]
    depth = w_mod.shape[0]
    mods_all = _mod_vectors(c, c_ctx, w_mod, b_mod).reshape(depth, 8, 6, d)
    smat = _smat()
    cos_x, sin_x = _rope_tables(seq)
    cos_c, sin_c = jnp.ones((lc, 128), F32), jnp.zeros((lc, 128), F32)

    x2d = x.reshape(nb * seq, d)
    xc2d = ctx.reshape(nb * lc, d)
    for l in range(depth):
        last = l == depth - 1
        p = _layer_params(l, w_in, na_q_gain, na_k_gain, na_rpb, sg_v_gain, sg_ws, sg_bs, ta_q_gain, ta_k_gain,
                          ta_w_out, pl_w, pl_scale, cv_dw_w, cv_dw_b, cv_norm_gain, cv_pw_w, cv_pw_b, w_out,
                          ff_w_up, ff_dw_w, ff_w_down)
        mx = [mods_all[l, :nb, k][:, None, :] for k in range(6)]
        mc = [jnp.broadcast_to(mods_all[l, nb, k][None, None, :], (nb, 1, d)) for k in range(6)]
        zc = _inproj(xc2d, mc[0], mc[1], p["w_in"], smat, p["gq"], p["gk"], p["gt"], cos_c, sin_c, tpb=lc)
        z = _inproj(x2d, mx[0], mx[1], p["w_in"], smat, p["gq"], p["gk"], p["gt"], cos_x, sin_x, tpb=seq)
        na_o = _na_x(z, zc, p["na_bias"], nb=nb, seq=seq, lc=lc)
        ta_o = _ta(z, zc, nb=nb, lq=seq, le=lc)
        ys = _mixers(p, z, na_o, ta_o, tpb=seq)
        if not last:
            na_oc = _na_ctx(zc, nb=nb, lc=lc)
            ta_oc = _ta(zc, None, nb=nb, lq=lc, le=0)
            ysc = _mixers(p, zc, na_oc, ta_oc, tpb=lc)
            xc2d = _tail(p, ysc, xc2d, mc, tpb=lc)
        x2d = _tail(p, ys, x2d, mx, tpb=seq)
    return x2d.reshape(nb, seq, d)
```

```python
import functools

import numpy as np
import jax
import jax.numpy as jnp
from jax import lax
from jax.experimental import pallas as pl
from jax.experimental.pallas import tpu as pltpu

F32 = jnp.float32
BF16 = jnp.bfloat16

D_MODEL = 2048
GRID_W = 64
HEAD_DIM = 64
W_GRP = D_MODEL // 4
NA_HEADS = W_GRP // HEAD_DIM
NA_WIN_R = 8
NA_WIN_C = 16
NA_QROWS = 4
NA_BAND = NA_WIN_R + NA_QROWS
NA_SUB = 8
LOG2E = 1.4426950408889634
SG_HEADS = 8
SG_CHUNK = 128
TA_DIM = 64
PL_WINDOWS = (2, 4, 8, 16)
CV_KERNEL = 31
FF_KERNEL = 3
ROPE_BASE = 10000.0
EPS = 1e-6
HALO = 16
SUBLANES = 8
IN_TILE = 512
N_IN_TILES = 9
Z_COLS = 8 * IN_TILE
ZB_NA_Q, ZB_NA_K, ZB_NA_V = 0, 4, 8
ZB_TA_Q, ZB_TA_K, ZB_TA_V = 28, 29, 30
ZT_SG_U, ZT_SG_V, ZT_POOL, ZT_CONV, ZT_TA = 3, 4, 5, 6, 7
NEG_BIG = -1e30
NT_DIMS = (((1,), (1,)), ((), ()))
VMEM_LIMIT_MB = 56


def _cparams(sem, vmem_mb=VMEM_LIMIT_MB):
    return pltpu.CompilerParams(dimension_semantics=sem, vmem_limit_bytes=vmem_mb * 1024 * 1024)


def _dot(a, b):
    return jnp.dot(a, b, preferred_element_type=F32)


def _dot_nt(a, b):
    return lax.dot_general(a, b, NT_DIMS, preferred_element_type=F32)


def _mod_kernel(c_ref, w_ref, b_ref, o_ref):
    cv = c_ref[...]
    s = cv * jax.nn.sigmoid(cv)
    o_ref[0] = _dot(s.astype(BF16), w_ref[0].astype(BF16)) + b_ref[0]


def _mod_vectors(c, c_ctx, w_mod, b_mod):
    depth, d, n = w_mod.shape
    nb = c.shape[0]
    assert nb + 1 <= 8
    cv = jnp.zeros((8, d), F32).at[:nb].set(c).at[nb].set(c_ctx)
    tn = 1024
    return pl.pallas_call(
        _mod_kernel,
        grid=(depth, n // tn),
        in_specs=[
            pl.BlockSpec((8, d), lambda l, k: (0, 0)),
            pl.BlockSpec((1, d, tn), lambda l, k: (l, 0, k)),
            pl.BlockSpec((1, 1, tn), lambda l, k: (l, 0, k)),
        ],
        out_specs=pl.BlockSpec((1, 8, tn), lambda l, k: (l, 0, k)),
        out_shape=jax.ShapeDtypeStruct((depth, 8, n), F32),
        compiler_params=_cparams(("parallel", "parallel"), 40),
        name="mod_vectors",
    )(cv, w_mod, b_mod.reshape(depth, 1, n))


def _rope(t, cos, sin):
    lane = lax.broadcasted_iota(jnp.int32, t.shape, 1)
    partner = jnp.where((lane % 32) < 16, pltpu.roll(t, 128 - 16, 1), pltpu.roll(t, 16, 1))
    return t * cos + partner * sin


def _inproj_kernel(x_ref, sh_ref, sc_ref, w_ref, smat_ref, gq_ref, gk_ref, gt_ref, cos_ref, sin_ref,
                   z_ref, xn_scr):
    half = x_ref.shape[0] // 2
    for r in range(2):
        _inproj_rows(slice(r * half, (r + 1) * half), x_ref, sh_ref, sc_ref, w_ref, smat_ref, gq_ref, gk_ref,
                     gt_ref, cos_ref, sin_ref, z_ref, xn_scr)


def _inproj_rows(rs, x_ref, sh_ref, sc_ref, w_ref, smat_ref, gq_ref, gk_ref, gt_ref, cos_ref, sin_ref,
                 z_ref, xn_scr):
    x = x_ref[rs, :]
    ms = jnp.mean(x * x, axis=-1, keepdims=True)
    xn_scr[rs, :] = (x * lax.rsqrt(ms + EPS) * (1.0 + sc_ref[0]) + sh_ref[0]).astype(BF16)

    def proj(j):
        return _dot(xn_scr[rs, :], w_ref[:, j * IN_TILE:(j + 1) * IN_TILE])

    def put(t, val):
        z_ref[rs, t * IN_TILE:(t + 1) * IN_TILE] = val.astype(BF16)

    def head_rms(a, gain):
        sq = a * a
        hi = sq.astype(BF16)
        lo = (sq - hi.astype(F32)).astype(BF16)
        ss = _dot(hi, smat_ref[...]) + _dot(lo, smat_ref[...])
        return a * lax.rsqrt(ss * (1.0 / HEAD_DIM) + EPS) * gain

    acc = proj(8)

    def nrm(t, g):
        ms_t = jnp.sum(t * t, axis=-1, keepdims=True) * (1.0 / TA_DIM)
        return t * lax.rsqrt(ms_t + EPS) * g
    cos = cos_ref[rs, :]
    sin = sin_ref[rs, :]
    tq = _rope(nrm(acc[:, 0:128], gt_ref[:, 0:128]), cos, sin) * (TA_DIM ** -0.5 * LOG2E)
    tk = _rope(nrm(acc[:, 128:256], gt_ref[:, 128:256]), cos, sin)
    tv = acc[:, 256:384]
    lane = lax.broadcasted_iota(jnp.int32, tv.shape, 1)
    c0 = ZT_TA * IN_TILE
    z_ref[rs, c0:c0 + 128] = tq.astype(BF16)
    z_ref[rs, c0 + 128:c0 + 256] = tk.astype(BF16)
    z_ref[rs, c0 + 256:c0 + 384] = jnp.where(lane == TA_DIM, 1.0, tv).astype(BF16)
    z_ref[rs, c0 + 384:c0 + 512] = acc[:, 384:512].astype(BF16)

    put(0, head_rms(proj(0), gq_ref[...]) * (HEAD_DIM ** -0.5 * LOG2E))
    put(1, head_rms(proj(1), gk_ref[...]))
    put(ZT_SG_U, jax.nn.gelu(proj(3)))
    put(ZT_SG_V, jax.nn.gelu(proj(4)))
    put(ZT_CONV, proj(6) * jax.nn.sigmoid(proj(7)))
    put(2, proj(2))
    put(ZT_POOL, proj(5))


def _inproj(x2d, sh, sc, w_r, smat, gq, gk, gt, cos_t, sin_t, *, tpb):
    t, d = x2d.shape
    tm = min(512, tpb)
    tiles_pb = tpb // tm
    const = lambda i: (0, 0)
    return pl.pallas_call(
        _inproj_kernel,
        grid=(t // tm,),
        in_specs=[
            pl.BlockSpec((tm, d), lambda i: (i, 0)),
            pl.BlockSpec((1, 1, d), lambda i: (i // tiles_pb, 0, 0)),
            pl.BlockSpec((1, 1, d), lambda i: (i // tiles_pb, 0, 0)),
            pl.BlockSpec((d, N_IN_TILES * IN_TILE), const, pipeline_mode=pl.Buffered(1)),
            pl.BlockSpec((IN_TILE, IN_TILE), const),
            pl.BlockSpec((1, IN_TILE), const),
            pl.BlockSpec((1, IN_TILE), const),
            pl.BlockSpec((1, 256), const),
            pl.BlockSpec((tm, 128), lambda i: (i % tiles_pb, 0)),
            pl.BlockSpec((tm, 128), lambda i: (i % tiles_pb, 0)),
        ],
        out_specs=pl.BlockSpec((tm, Z_COLS), lambda i: (i, 0)),
        out_shape=jax.ShapeDtypeStruct((t, Z_COLS), BF16),
        scratch_shapes=[pltpu.VMEM((tm, d), BF16)],
        compiler_params=_cparams(("parallel",)),
        name="in_proj",
    )(x2d, sh, sc, w_r, smat, gq, gk, gt, cos_t, sin_t)


def _stack_heads(q):
    lane = lax.broadcasted_iota(jnp.int32, q.shape, 1)
    zero = jnp.zeros_like(q)
    return jnp.concatenate([jnp.where(lane < HEAD_DIM, q, zero), jnp.where(lane >= HEAD_DIM, q, zero)], axis=0)


def _unstack_heads(o2):
    m = o2.shape[0] // 2
    lane = lax.broadcasted_iota(jnp.int32, (m, 128), 1)
    return jnp.where(lane < HEAD_DIM, o2[:m], o2[m:])


def _na_x_kernel(q_ref, k_ref, v_ref, kc_ref, vc_ref, b_ref, o_ref, *, rows):
    tq = NA_QROWS * GRID_W
    nkeys = NA_BAND * GRID_W
    nrb = rows // NA_QROWS
    kc = kc_ref[...]
    vc = vc_ref[...]
    for sub in range(NA_SUB):
        rb = pl.program_id(2) * NA_SUB + sub
        rs = jnp.clip(rb * NA_QROWS - NA_WIN_R // 2, 0, rows - NA_BAND)
        start = pl.multiple_of(rs * GRID_W, GRID_W)
        case = jnp.where(rb == 0, 0, jnp.where(rb == nrb - 1, 2, 1))
        kw = k_ref[pl.ds(start, nkeys), :]
        vw = v_ref[pl.ds(start, nkeys), :]
        q2 = _stack_heads(q_ref[sub * tq:(sub + 1) * tq, :])
        s1 = _dot_nt(q2, kw) + b_ref[case, 0]
        s2 = _dot_nt(q2, kc)
        m = jnp.maximum(jnp.max(s1, axis=-1, keepdims=True), jnp.max(s2, axis=-1, keepdims=True))
        p1 = jnp.exp2(s1 - m)
        p2 = jnp.exp2(s2 - m)
        l = jnp.sum(p1, axis=-1, keepdims=True) + jnp.sum(p2, axis=-1, keepdims=True)
        o2 = _dot(p1.astype(BF16), vw) + _dot(p2.astype(BF16), vc)
        o_ref[sub * tq:(sub + 1) * tq, :] = _unstack_heads(o2 / l).astype(BF16)


def _na_x(z, zc, bias, *, nb, seq, lc):
    rows = seq // GRID_W
    assert rows % (NA_QROWS * NA_SUB) == 0 and rows >= NA_BAND
    nsteps = rows // (NA_QROWS * NA_SUB)
    tq = NA_QROWS * GRID_W
    return pl.pallas_call(
        functools.partial(_na_x_kernel, rows=rows),
        grid=(nb, NA_HEADS // 2, nsteps),
        in_specs=[
            pl.BlockSpec((NA_SUB * tq, 128), lambda b, hp, st: (b * nsteps + st, ZB_NA_Q + hp)),
            pl.BlockSpec((seq, 128), lambda b, hp, st: (b, ZB_NA_K + hp)),
            pl.BlockSpec((seq, 128), lambda b, hp, st: (b, ZB_NA_V + hp)),
            pl.BlockSpec((lc, 128), lambda b, hp, st: (b, ZB_NA_K + hp)),
            pl.BlockSpec((lc, 128), lambda b, hp, st: (b, ZB_NA_V + hp)),
            pl.BlockSpec((3, 1, 2 * tq, NA_BAND * GRID_W), lambda b, hp, st: (0, hp, 0, 0)),
        ],
        out_specs=pl.BlockSpec((NA_SUB * tq, 128), lambda b, hp, st: (b * nsteps + st, hp)),
        out_shape=jax.ShapeDtypeStruct((nb * seq, W_GRP), BF16),
        compiler_params=_cparams(("parallel", "parallel", "arbitrary")),
        name="na_attn",
    )(z, z, z, zc, zc, bias)


def _na_bias_table(rpb):
    h = rpb.shape[0]
    nq = NA_QROWS * GRID_W
    nk = NA_BAND * GRID_W
    qc, kc = np.arange(GRID_W), np.arange(GRID_W)
    c0 = np.clip(qc - NA_WIN_C // 2, 0, GRID_W - NA_WIN_C)
    col_ok = (kc[None, :] >= c0[:, None]) & (kc[None, :] < c0[:, None] + NA_WIN_C)
    cidx = np.clip(kc[None, :] - qc[:, None] + NA_WIN_C - 1, 0, 2 * NA_WIN_C - 2)
    col_sel = jnp.asarray((cidx.reshape(-1)[None, :] == np.arange(2 * NA_WIN_C - 1)[:, None]).astype(np.float32))
    qr, wr = np.arange(NA_QROWS), np.arange(NA_BAND)
    half = NA_WIN_R // 2
    cases = ((0, np.zeros(NA_QROWS, int)), (half, np.arange(NA_QROWS)), (NA_WIN_R, np.full(NA_QROWS, half)))
    tabs = []
    for dq, r0 in cases:
        row_ok = (wr[None, :] >= r0[:, None]) & (wr[None, :] < r0[:, None] + NA_WIN_R)
        ridx = np.clip(wr[None, :] - (dq + qr[:, None]) + NA_WIN_R - 1, 0, 2 * NA_WIN_R - 2)
        rows = rpb[:, ridx.reshape(-1), :]
        blocks = jnp.einsum("hrc,cx->hrx", rows, col_sel, precision=lax.Precision.HIGHEST)
        blocks = blocks.reshape(h, NA_QROWS, NA_BAND, GRID_W, GRID_W).transpose(0, 1, 3, 2, 4)
        valid = row_ok[:, None, :, None] & col_ok[None, :, None, :]
        tabs.append(jnp.where(jnp.asarray(valid)[None], blocks * LOG2E, NEG_BIG).reshape(h, nq, nk))
    tab = jnp.stack(tabs)
    return tab.reshape(3, h // 2, 2 * nq, nk).astype(F32)


def _na_ctx_kernel(q_ref, k_ref, v_ref, o_ref):
    q2 = _stack_heads(q_ref[...])
    s = _dot_nt(q2, k_ref[...])
    m = jnp.max(s, axis=-1, keepdims=True)
    p = jnp.exp2(s - m)
    l = jnp.sum(p, axis=-1, keepdims=True)
    o2 = _dot(p.astype(BF16), v_ref[...])
    o_ref[...] = _unstack_heads(o2 / l).astype(BF16)


def _na_ctx(zc, *, nb, lc):
    return pl.pallas_call(
        _na_ctx_kernel,
        grid=(nb, NA_HEADS // 2),
        in_specs=[
            pl.BlockSpec((lc, 128), lambda b, hp: (b, ZB_NA_Q + hp)),
            pl.BlockSpec((lc, 128), lambda b, hp: (b, ZB_NA_K + hp)),
            pl.BlockSpec((lc, 128), lambda b, hp: (b, ZB_NA_V + hp)),
        ],
        out_specs=pl.BlockSpec((lc, 128), lambda b, hp: (b, hp)),
        out_shape=jax.ShapeDtypeStruct((nb * lc, W_GRP), BF16),
        compiler_params=_cparams(("parallel", "parallel")),
        name="na_attn_ctx",
    )(zc, zc, zc)


def _ta_kernel(*refs, tk, nk, has_extra):
    if has_extra:
        q_ref, k_ref, v_ref, ke_ref, ve_ref, o_ref = refs
    else:
        q_ref, k_ref, v_ref, o_ref = refs
    q = q_ref[...]
    tq = q.shape[0]

    def step(kb, vb, carry):
        m, acc = carry
        s = _dot_nt(q, kb)
        mn = jnp.maximum(m, jnp.max(s, axis=-1, keepdims=True))
        p = jnp.exp2(s - mn).astype(BF16)
        acc = jnp.exp2(m - mn) * acc + _dot(p, vb)
        return mn, acc

    def body(c, carry):
        st = pl.multiple_of(c * tk, tk)
        return step(k_ref[pl.ds(st, tk), :], v_ref[pl.ds(st, tk), :], carry)

    carry = (jnp.full((tq, 1), NEG_BIG, F32), jnp.zeros((tq, 128), F32))
    carry = lax.fori_loop(0, nk, body, carry, unroll=8 if nk % 8 == 0 else 1)
    if has_extra:
        carry = step(ke_ref[...], ve_ref[...], carry)
    _, acc = carry
    lane = lax.broadcasted_iota(jnp.int32, acc.shape, 1)
    l = jnp.sum(jnp.where(lane == TA_DIM, acc, 0.0), axis=-1, keepdims=True)
    o_ref[...] = jnp.where(lane < TA_DIM, acc / l, 0.0).astype(BF16)


def _ta(zq, zextra, *, nb, lq, le):
    tq = min(512, lq)
    tk = min(1024, lq)
    nqb = lq // tq
    has_extra = zextra is not None
    in_specs = [
        pl.BlockSpec((tq, 128), lambda b, qi: (b * nqb + qi, ZB_TA_Q)),
        pl.BlockSpec((lq, 128), lambda b, qi: (b, ZB_TA_K)),
        pl.BlockSpec((lq, 128), lambda b, qi: (b, ZB_TA_V)),
    ]
    args = [zq, zq, zq]
    if has_extra:
        in_specs += [
            pl.BlockSpec((le, 128), lambda b, qi: (b, ZB_TA_K)),
            pl.BlockSpec((le, 128), lambda b, qi: (b, ZB_TA_V)),
        ]
        args += [zextra, zextra]
    return pl.pallas_call(
        functools.partial(_ta_kernel, tk=tk, nk=lq // tk, has_extra=has_extra),
        grid=(nb, nqb),
        in_specs=in_specs,
        out_specs=pl.BlockSpec((tq, 128), lambda b, qi: (b * nqb + qi, 0)),
        out_shape=jax.ShapeDtypeStruct((nb * lq, 128), BF16),
        compiler_params=_cparams(("parallel", "arbitrary")),
        name="ta_attn",
    )(*args)


def _sg_kernel(u_ref, gv_ref, tao_ref, gain_ref, wcat_ref, bsx_ref, two_ref, o_ref, v_scr):
    tm = u_ref.shape[0]
    gv = gv_ref[...].astype(F32)
    ms = jnp.mean(gv * gv, axis=-1, keepdims=True)
    v_scr[...] = (gv * lax.rsqrt(ms + EPS) * gain_ref[...]).astype(BF16)
    lane = lax.broadcasted_iota(jnp.int32, (SG_CHUNK, 128), 1)
    zero = jnp.zeros((SG_CHUNK, 128), BF16)

    def chunk(c, carry):
        r0 = pl.multiple_of(c * SG_CHUNK, SG_CHUNK)
        rows = pl.ds(r0, SG_CHUNK)
        outs = []
        for p in range(SG_HEADS // 2):
            vs = v_scr[rows, p * 128:(p + 1) * 128]
            rhs = jnp.concatenate([jnp.where(lane < 64, vs, zero), jnp.where(lane >= 64, vs, zero)], axis=0)
            outs.append(_dot(wcat_ref[p], rhs))
        s = jnp.concatenate(outs, axis=1) + bsx_ref[...] + _dot(tao_ref[rows, :], two_ref[...])
        o_ref[rows, :] = (u_ref[rows, :].astype(F32) * s).astype(BF16)
        return carry

    lax.fori_loop(0, tm // SG_CHUNK, chunk, 0)


def _sg(z, tao, gain, wcat, bsx, two, *, tpb):
    t = z.shape[0]
    tm = min(512, tpb)
    const2 = lambda i: (0, 0)
    return pl.pallas_call(
        _sg_kernel,
        grid=(t // tm,),
        in_specs=[
            pl.BlockSpec((tm, W_GRP), lambda i: (i, ZT_SG_U)),
            pl.BlockSpec((tm, W_GRP), lambda i: (i, ZT_SG_V)),
            pl.BlockSpec((tm, 128), lambda i: (i, 0)),
            pl.BlockSpec((1, W_GRP), const2),
            pl.BlockSpec((SG_HEADS // 2, SG_CHUNK, 2 * SG_CHUNK), lambda i: (0, 0, 0)),
            pl.BlockSpec((SG_CHUNK, W_GRP), const2),
            pl.BlockSpec((128, W_GRP), const2),
        ],
        out_specs=pl.BlockSpec((tm, W_GRP), lambda i: (i, 0)),
        out_shape=jax.ShapeDtypeStruct((t, W_GRP), BF16),
        scratch_shapes=[pltpu.VMEM((tm, W_GRP), BF16)],
        compiler_params=_cparams(("parallel",)),
        name="sg_unit",
    )(z, z, tao, gain, wcat, bsx, two)


def _halo_specs(tm, t, width, col):
    hb = tm // HALO
    last = t // HALO - 1
    return [
        pl.BlockSpec((tm, width), lambda i: (i, col)),
        pl.BlockSpec((HALO, width), lambda i: (jnp.maximum(i * hb - 1, 0), col)),
        pl.BlockSpec((HALO, width), lambda i: (jnp.minimum((i + 1) * hb, last), col)),
    ]


def _fill_ext(ext_scr, main_ref, prev_ref, next_ref, it, tiles_pb):
    tm = main_ref.shape[0]
    zero = jnp.zeros(prev_ref.shape, F32)
    ext_scr[0:HALO, :] = jnp.where(it == 0, zero, prev_ref[...].astype(F32))
    ext_scr[HALO:HALO + tm, :] = main_ref[...].astype(F32)
    ext_scr[HALO + tm:2 * HALO + tm, :] = jnp.where(it == tiles_pb - 1, zero, next_ref[...].astype(F32))


def _pool_kernel(z_ref, zp_ref, zn_ref, wbd_ref, scale_ref, o_ref, ext_scr, d_scr, *, tiles_pb, seq):
    tm = z_ref.shape[0]
    it = pl.program_id(0) % tiles_pb
    _fill_ext(ext_scr, z_ref, zp_ref, zn_ref, it, tiles_pb)
    rc = 128

    for c in range(tm // rc):
        r0 = c * rc
        pos = it * tm + r0 + lax.broadcasted_iota(jnp.int32, (rc, 1), 0)
        for g, w in enumerate(PL_WINDOWS):
            cols = slice(g * 128, (g + 1) * 128)
            lo = r0 + HALO - w // 2
            acc = ext_scr[lo:lo + rc, cols]
            for s in range(1, w):
                acc = acc + ext_scr[lo + s:lo + s + rc, cols]
            cnt = (jnp.minimum(pos + w // 2, seq) - jnp.maximum(pos - w // 2, 0)).astype(F32)
            d = acc / cnt - ext_scr[r0 + HALO:r0 + HALO + rc, cols]
            d_scr[r0:r0 + rc, cols] = d.astype(BF16)
    o_ref[...] = (_dot(d_scr[...], wbd_ref[...]) * scale_ref[...]).astype(BF16)


def _pool(z, wbd, scale, *, tpb):
    t = z.shape[0]
    tm = min(512, tpb)
    const2 = lambda i: (0, 0)
    return pl.pallas_call(
        functools.partial(_pool_kernel, tiles_pb=tpb // tm, seq=tpb),
        grid=(t // tm,),
        in_specs=_halo_specs(tm, t, W_GRP, ZT_POOL) + [
            pl.BlockSpec((W_GRP, W_GRP), const2),
            pl.BlockSpec((1, W_GRP), const2),
        ],
        out_specs=pl.BlockSpec((tm, W_GRP), lambda i: (i, 0)),
        out_shape=jax.ShapeDtypeStruct((t, W_GRP), BF16),
        scratch_shapes=[pltpu.VMEM((tm + 2 * HALO, W_GRP), F32), pltpu.VMEM((tm, W_GRP), BF16)],
        compiler_params=_cparams(("parallel",)),
        name="pool_mixer",
    )(z, z, z, wbd, scale)


def _conv_kernel(h_ref, hp_ref, hn_ref, dww_ref, dwb_ref, ng_ref, pww_ref, pwb_ref, o_ref, ext_scr, sh_scr,
                 a_scr, *, tiles_pb):
    tm = h_ref.shape[0]
    it = pl.program_id(0) % tiles_pb
    _fill_ext(ext_scr, h_ref, hp_ref, hn_ref, it, tiles_pb)
    rc = 64
    base = HALO - (CV_KERNEL - 1) // 2
    nsh = sh_scr.shape[1]
    for s in range(1, SUBLANES):
        sh_scr[s - 1] = ext_scr[s:s + nsh, :]

    for c in range(tm // rc):
        r0 = c * rc
        acc = jnp.broadcast_to(dwb_ref[...], (rc, W_GRP))
        for k in range(CV_KERNEL):
            a, s = divmod(base + k, SUBLANES)
            lo = r0 + a * SUBLANES
            rows = ext_scr[lo:lo + rc, :] if s == 0 else sh_scr[s - 1, lo:lo + rc, :]
            acc = acc + dww_ref[k:k + 1, :] * rows
        ms = jnp.mean(acc * acc, axis=-1, keepdims=True)
        hh = acc * lax.rsqrt(ms + EPS) * ng_ref[...]
        a_scr[r0:r0 + rc, :] = (hh * jax.nn.sigmoid(hh)).astype(BF16)
    o_ref[...] = (_dot(a_scr[...], pww_ref[...]) + pwb_ref[...]).astype(BF16)


def _conv(z, dww, dwb, ng, pww, pwb, *, tpb):
    t = z.shape[0]
    tm = min(512, tpb)
    const2 = lambda i: (0, 0)
    return pl.pallas_call(
        functools.partial(_conv_kernel, tiles_pb=tpb // tm),
        grid=(t // tm,),
        in_specs=_halo_specs(tm, t, W_GRP, ZT_CONV) + [
            pl.BlockSpec((CV_KERNEL, W_GRP), const2),
            pl.BlockSpec((1, W_GRP), const2),
            pl.BlockSpec((1, W_GRP), const2),
            pl.BlockSpec((W_GRP, W_GRP), const2),
            pl.BlockSpec((1, W_GRP), const2),
        ],
        out_specs=pl.BlockSpec((tm, W_GRP), lambda i: (i, 0)),
        out_shape=jax.ShapeDtypeStruct((t, W_GRP), BF16),
        scratch_shapes=[pltpu.VMEM((tm + 2 * HALO, W_GRP), F32),
                        pltpu.VMEM((SUBLANES - 1, tm + 2 * HALO - SUBLANES, W_GRP), F32),
                        pltpu.VMEM((tm, W_GRP), BF16)],
        compiler_params=_cparams(("parallel",)),
        name="conformer_conv",
    )(z, z, z, dww, dwb, ng, pww, pwb)


def _outproj_kernel(yna_ref, ysg_ref, ypl_ref, ycv_ref, w_ref, x_ref, g1_ref, sh2_ref, sc2_ref, x1_ref, hn_ref):
    tm, d = x_ref.shape
    y = jnp.concatenate([yna_ref[...], ysg_ref[...], ypl_ref[...], ycv_ref[...]], axis=1)
    nc = 512
    ssq = jnp.zeros((tm, 1), F32)
    for n in range(d // nc):
        cs = slice(n * nc, (n + 1) * nc)
        x1 = x_ref[:, cs] + g1_ref[0][:, cs] * _dot(y, w_ref[:, cs])
        x1_ref[:, cs] = x1
        ssq = ssq + jnp.sum(x1 * x1, axis=-1, keepdims=True)
    r = lax.rsqrt(ssq * (1.0 / d) + EPS)
    for n in range(d // nc):
        cs = slice(n * nc, (n + 1) * nc)
        hn_ref[:, cs] = (x1_ref[:, cs] * r * (1.0 + sc2_ref[0][:, cs]) + sh2_ref[0][:, cs]).astype(BF16)


def _outproj(ys, w_out, x2d, g1, sh2, sc2, *, tpb):
    t, d = x2d.shape
    tm = min(512, tpb)
    tiles_pb = tpb // tm
    row = lambda i: (i // tiles_pb, 0, 0)
    return pl.pallas_call(
        _outproj_kernel,
        grid=(t // tm,),
        in_specs=[pl.BlockSpec((tm, W_GRP), lambda i: (i, 0))] * 4 + [
            pl.BlockSpec((d, d), lambda i: (0, 0)),
            pl.BlockSpec((tm, d), lambda i: (i, 0)),
            pl.BlockSpec((1, 1, d), row),
            pl.BlockSpec((1, 1, d), row),
            pl.BlockSpec((1, 1, d), row),
        ],
        out_specs=[pl.BlockSpec((tm, d), lambda i: (i, 0)), pl.BlockSpec((tm, d), lambda i: (i, 0))],
        out_shape=[jax.ShapeDtypeStruct((t, d), F32), jax.ShapeDtypeStruct((t, d), BF16)],
        compiler_params=_cparams(("parallel",)),
        name="out_proj",
    )(*ys, w_out, x2d, g1, sh2, sc2)


def _ffn_kernel(hn_ref, hp_ref, hx_ref, x1_ref, g2_ref, wg_ref, wv_ref, wd_ref, dw_ref, o_ref, ext_scr, g_scr,
                *, tiles_pb, nj):
    tm, d = hn_ref.shape
    j = pl.program_id(1)
    it = pl.program_id(0) % tiles_pb

    @pl.when(j == 0)
    def _():
        zero = jnp.zeros((HALO, d), BF16)
        ext_scr[0:HALO, :] = jnp.where(it == 0, zero, hp_ref[...])
        ext_scr[HALO:HALO + tm, :] = hn_ref[...]
        ext_scr[HALO + tm:2 * HALO + tm, :] = jnp.where(it == tiles_pb - 1, zero, hx_ref[...])
        o_ref[...] = jnp.zeros(o_ref.shape, F32)

    g_scr[...] = _dot(ext_scr[...], wg_ref[...])
    val = _dot(hn_ref[...], wv_ref[...])
    c = (dw_ref[0:1, :] * g_scr[HALO - 1:HALO - 1 + tm, :] + dw_ref[1:2, :] * g_scr[HALO:HALO + tm, :]
         + dw_ref[2:3, :] * g_scr[HALO + 1:HALO + 1 + tm, :])
    a = (c * jax.nn.sigmoid(c) * val).astype(BF16)
    nc = 512
    for n in range(d // nc):
        cs = slice(n * nc, (n + 1) * nc)
        o_ref[:, cs] += _dot(a, wd_ref[:, cs])

    @pl.when(j == nj - 1)
    def _():
        o_ref[...] = x1_ref[...] + g2_ref[0] * o_ref[...]


def _ffn(hn, x1, g2, wg, wv, wd, dw, *, tpb):
    t, d = x1.shape
    ff = wg.shape[1]
    tm = min(1024, tpb)
    tiles_pb = tpb // tm
    tf = 512
    nj = ff // tf
    hb = tm // HALO
    last = t // HALO - 1
    once = pl.Buffered(1)
    return pl.pallas_call(
        functools.partial(_ffn_kernel, tiles_pb=tiles_pb, nj=nj),
        grid=(t // tm, nj),
        in_specs=[
            pl.BlockSpec((tm, d), lambda i, j: (i, 0), pipeline_mode=once),
            pl.BlockSpec((HALO, d), lambda i, j: (jnp.maximum(i * hb - 1, 0), 0)),
            pl.BlockSpec((HALO, d), lambda i, j: (jnp.minimum((i + 1) * hb, last), 0)),
            pl.BlockSpec((tm, d), lambda i, j: (i, 0), pipeline_mode=once),
            pl.BlockSpec((1, 1, d), lambda i, j: (i // tiles_pb, 0, 0)),
            pl.BlockSpec((d, tf), lambda i, j: (0, j)),
            pl.BlockSpec((d, tf), lambda i, j: (0, j)),
            pl.BlockSpec((tf, d), lambda i, j: (j, 0)),
            pl.BlockSpec((FF_KERNEL, tf), lambda i, j: (0, j)),
        ],
        out_specs=pl.BlockSpec((tm, d), lambda i, j: (i, 0)),
        out_shape=jax.ShapeDtypeStruct((t, d), F32),
        scratch_shapes=[pltpu.VMEM((tm + 2 * HALO, d), BF16), pltpu.VMEM((tm + 2 * HALO, tf), F32)],
        compiler_params=_cparams(("parallel", "arbitrary")),
        name="conv_glu_ffn",
    )(hn, hn, hn, x1, g2, wg, wv, wd, dw)


def _rope_tables(seq):
    half = TA_DIM // 2
    freqs = ROPE_BASE ** (-np.arange(0, half, 2, dtype=np.float32) / half)
    pos = np.arange(seq)
    lane = np.arange(TA_DIM)
    p = np.where(lane[None, :] < half, (pos // GRID_W)[:, None], (pos % GRID_W)[:, None]).astype(np.float32)
    m = lane % half
    ang = jnp.asarray(p * freqs[m % (half // 2)][None, :])
    sign = jnp.asarray(np.where(m < half // 2, -1.0, 1.0).astype(np.float32))[None, :]
    cos = jnp.concatenate([jnp.cos(ang), jnp.ones((seq, 128 - TA_DIM), F32)], axis=1)
    sin = jnp.concatenate([jnp.sin(ang) * sign, jnp.zeros((seq, 128 - TA_DIM), F32)], axis=1)
    return cos, sin


def _layout_w_in(w):
    d = w.shape[0]
    g = W_GRP
    na, sg = w[:, 0:3 * g], w[:, 3 * g:5 * g]
    o = 5 * g
    tq, tk, tv = w[:, o:o + TA_DIM], w[:, o + TA_DIM:o + 2 * TA_DIM], w[:, o + 2 * TA_DIM:o + 3 * TA_DIM]
    rest = w[:, o + 3 * TA_DIM:]
    z64 = jnp.zeros((d, 128 - TA_DIM), w.dtype)
    ta = jnp.concatenate([tq, z64, tk, z64, tv, jnp.zeros((d, IN_TILE - 3 * 128 + 128 - TA_DIM), w.dtype)], axis=1)
    return jnp.concatenate([na, sg, rest, ta], axis=1).astype(BF16)


def _layer_params(l, w_in, na_q_gain, na_k_gain, na_rpb, sg_v_gain, sg_ws, sg_bs, ta_q_gain, ta_k_gain,
                  ta_w_out, pl_w, pl_scale, cv_dw_w, cv_dw_b, cv_norm_gain, cv_pw_w, cv_pw_b, w_out,
                  ff_w_up, ff_dw_w, ff_w_down):
    ones64 = jnp.ones((128 - TA_DIM,), F32)
    ws = sg_ws[l].astype(BF16)
    ff = ff_w_down.shape[1]
    p = dict(
        w_in=_layout_w_in(w_in[l]),
        gq=jnp.tile(na_q_gain[l], NA_HEADS)[None, :],
        gk=jnp.tile(na_k_gain[l], NA_HEADS)[None, :],
        gt=jnp.concatenate([ta_q_gain[l], ones64, ta_k_gain[l], ones64])[None, :],
        na_bias=_na_bias_table(na_rpb[l]),
        sg_gain=sg_v_gain[l][None, :],
        sg_wcat=jnp.concatenate([ws[0::2], ws[1::2]], axis=2),
        sg_bsx=jnp.repeat(sg_bs[l].T, W_GRP // SG_HEADS, axis=1),
        ta_w_out=jnp.concatenate([ta_w_out[l], jnp.zeros((128 - TA_DIM, W_GRP), F32)], axis=0).astype(BF16),
        pl_wbd=jax.scipy.linalg.block_diag(*[pl_w[l, g] for g in range(len(PL_WINDOWS))]).astype(BF16),
        pl_scale=pl_scale[l][None, :],
        cv_dw_w=cv_dw_w[l], cv_dw_b=cv_dw_b[l][None, :], cv_ng=cv_norm_gain[l][None, :],
        cv_pw_w=cv_pw_w[l].astype(BF16), cv_pw_b=cv_pw_b[l][None, :],
        w_out=w_out[l].astype(BF16),
        ff_wg=ff_w_up[l, :, :ff].astype(BF16), ff_wv=ff_w_up[l, :, ff:].astype(BF16),
        ff_dw=ff_dw_w[l], ff_wd=ff_w_down[l].astype(BF16),
    )
    return p


def _smat():
    idx = np.arange(IN_TILE) // HEAD_DIM
    return jnp.asarray((idx[:, None] == idx[None, :]).astype(np.float32)).astype(BF16)


def _mixers(p, z, na_o, ta_o, *, tpb):
    sg_o = _sg(z, ta_o, p["sg_gain"], p["sg_wcat"], p["sg_bsx"], p["ta_w_out"], tpb=tpb)
    pl_o = _pool(z, p["pl_wbd"], p["pl_scale"], tpb=tpb)
    cv_o = _conv(z, p["cv_dw_w"], p["cv_dw_b"], p["cv_ng"], p["cv_pw_w"], p["cv_pw_b"], tpb=tpb)
    return [na_o, sg_o, pl_o, cv_o]


def _tail(p, ys, x2d, mods, *, tpb):
    x1, hn = _outproj(ys, p["w_out"], x2d, mods[2], mods[3], mods[4], tpb=tpb)
    return _ffn(hn, x1, mods[5], p["ff_wg"], p["ff_wv"], p["ff_wd"], p["ff_dw"], tpb=tpb)


def kernel(x, c, ctx, c_ctx, w_mod, b_mod, w_in, na_q_gain, na_k_gain, na_rpb, sg_v_gain, sg_ws, sg_bs,
           ta_q_gain, ta_k_gain, ta_w_out, pl_w, pl_scale, cv_dw_w, cv_dw_b, cv_norm_gain, cv_pw_w, cv_pw_b,
           w_out, ff_w_up, ff_dw_w, ff_w_down):
    nb, seq, d = x.shape
    lc = ctx.shape[1]
    depth = w_mod.shape[0]
    mods_all = _mod_vectors(c, c_ctx, w_mod, b_mod).reshape(depth, 8, 6, d)
    smat = _smat()
    cos_x, sin_x = _rope_tables(seq)
    cos_c, sin_c = jnp.ones((lc, 128), F32), jnp.zeros((lc, 128), F32)

    x2d = x.reshape(nb * seq, d)
    xc2d = ctx.reshape(nb * lc, d)
    for l in range(depth):
        last = l == depth - 1
        p = _layer_params(l, w_in, na_q_gain, na_k_gain, na_rpb, sg_v_gain, sg_ws, sg_bs, ta_q_gain, ta_k_gain,
                          ta_w_out, pl_w, pl_scale, cv_dw_w, cv_dw_b, cv_norm_gain, cv_pw_w, cv_pw_b, w_out,
                          ff_w_up, ff_dw_w, ff_w_down)
        mx = [mods_all[l, :nb, k][:, None, :] for k in range(6)]
        mc = [jnp.broadcast_to(mods_all[l, nb, k][None, None, :], (nb, 1, d)) for k in range(6)]
        zc = _inproj(xc2d, mc[0], mc[1], p["w_in"], smat, p["gq"], p["gk"], p["gt"], cos_c, sin_c, tpb=lc)
        z = _inproj(x2d, mx[0], mx[1], p["w_in"], smat, p["gq"], p["gk"], p["gt"], cos_x, sin_x, tpb=seq)
        na_o = _na_x(z, zc, p["na_bias"], nb=nb, seq=seq, lc=lc)
        ta_o = _ta(z, zc, nb=nb, lq=seq, le=lc)
        ys = _mixers(p, z, na_o, ta_o, tpb=seq)
        if not last:
            na_oc = _na_ctx(zc, nb=nb, lc=lc)
            ta_oc = _ta(zc, None, nb=nb, lq=lc, le=0)
            ysc = _mixers(p, zc, na_oc, ta_oc, tpb=lc)
            xc2d = _tail(p, ysc, xc2d, mc, tpb=lc)
        x2d = _tail(p, ys, x2d, mx, tpb=seq)
    return x2d.reshape(nb, seq, d)
```

```python
import functools

import numpy as np
import jax
import jax.numpy as jnp
from jax import lax
from jax.experimental import pallas as pl
from jax.experimental.pallas import tpu as pltpu

F32 = jnp.float32
BF16 = jnp.bfloat16

D_MODEL = 2048
GRID_W = 64
HEAD_DIM = 64
W_GRP = D_MODEL // 4
NA_HEADS = W_GRP // HEAD_DIM
NA_WIN_R = 8
NA_WIN_C = 16
NA_QROWS = 4
NA_BAND = NA_WIN_R + NA_QROWS
NA_SUB = 8
LOG2E = 1.4426950408889634
SG_HEADS = 8
SG_CHUNK = 128
TA_DIM = 64
PL_WINDOWS = (2, 4, 8, 16)
CV_KERNEL = 31
FF_KERNEL = 3
ROPE_BASE = 10000.0
EPS = 1e-6
HALO = 16
SUBLANES = 8
IN_TILE = 512
N_IN_TILES = 9
Z_COLS = 8 * IN_TILE
ZB_NA_Q, ZB_NA_K, ZB_NA_V = 0, 4, 8
ZB_TA_Q, ZB_TA_K, ZB_TA_V = 28, 29, 30
ZT_SG_U, ZT_SG_V, ZT_POOL, ZT_CONV, ZT_TA = 3, 4, 5, 6, 7
NEG_BIG = -1e30
NT_DIMS = (((1,), (1,)), ((), ()))
VMEM_LIMIT_MB = 56


def _cparams(sem, vmem_mb=VMEM_LIMIT_MB):
    return pltpu.CompilerParams(dimension_semantics=sem, vmem_limit_bytes=vmem_mb * 1024 * 1024)


def _dot(a, b):
    return jnp.dot(a, b, preferred_element_type=F32)


def _dot_nt(a, b):
    return lax.dot_general(a, b, NT_DIMS, preferred_element_type=F32)


def _mod_kernel(c_ref, w_ref, b_ref, o_ref):
    cv = c_ref[...]
    s = cv * jax.nn.sigmoid(cv)
    o_ref[0] = _dot(s.astype(BF16), w_ref[0].astype(BF16)) + b_ref[0]


def _mod_vectors(c, c_ctx, w_mod, b_mod):
    depth, d, n = w_mod.shape
    nb = c.shape[0]
    assert nb + 1 <= 8
    cv = jnp.zeros((8, d), F32).at[:nb].set(c).at[nb].set(c_ctx)
    tn = 1024
    return pl.pallas_call(
        _mod_kernel,
        grid=(depth, n // tn),
        in_specs=[
            pl.BlockSpec((8, d), lambda l, k: (0, 0)),
            pl.BlockSpec((1, d, tn), lambda l, k: (l, 0, k)),
            pl.BlockSpec((1, 1, tn), lambda l, k: (l, 0, k)),
        ],
        out_specs=pl.BlockSpec((1, 8, tn), lambda l, k: (l, 0, k)),
        out_shape=jax.ShapeDtypeStruct((depth, 8, n), F32),
        compiler_params=_cparams(("parallel", "parallel"), 40),
        name="mod_vectors",
    )(cv, w_mod, b_mod.reshape(depth, 1, n))


def _rope(t, cos, sin):
    lane = lax.broadcasted_iota(jnp.int32, t.shape, 1)
    partner = jnp.where((lane % 32) < 16, pltpu.roll(t, 128 - 16, 1), pltpu.roll(t, 16, 1))
    return t * cos + partner * sin


def _inproj_kernel(x_ref, sh_ref, sc_ref, w_ref, smat_ref, gq_ref, gk_ref, gt_ref, cos_ref, sin_ref,
                   z_ref, xn_scr):
    half = x_ref.shape[0] // 2
    for r in range(2):
        _inproj_rows(slice(r * half, (r + 1) * half), x_ref, sh_ref, sc_ref, w_ref, smat_ref, gq_ref, gk_ref,
                     gt_ref, cos_ref, sin_ref, z_ref, xn_scr)


def _inproj_rows(rs, x_ref, sh_ref, sc_ref, w_ref, smat_ref, gq_ref, gk_ref, gt_ref, cos_ref, sin_ref,
                 z_ref, xn_scr):
    x = x_ref[rs, :]
    ms = jnp.mean(x * x, axis=-1, keepdims=True)
    xn_scr[rs, :] = (x * lax.rsqrt(ms + EPS) * (1.0 + sc_ref[0]) + sh_ref[0]).astype(BF16)

    def proj(j):
        return _dot(xn_scr[rs, :], w_ref[:, j * IN_TILE:(j + 1) * IN_TILE])

    def put(t, val):
        z_ref[rs, t * IN_TILE:(t + 1) * IN_TILE] = val.astype(BF16)

    def head_rms(a, gain):
        sq = a * a
        hi = sq.astype(BF16)
        lo = (sq - hi.astype(F32)).astype(BF16)
        ss = _dot(hi, smat_ref[...]) + _dot(lo, smat_ref[...])
        return a * lax.rsqrt(ss * (1.0 / HEAD_DIM) + EPS) * gain

    acc = proj(8)

    def nrm(t, g):
        ms_t = jnp.sum(t * t, axis=-1, keepdims=True) * (1.0 / TA_DIM)
        return t * lax.rsqrt(ms_t + EPS) * g
    cos = cos_ref[rs, :]
    sin = sin_ref[rs, :]
    tq = _rope(nrm(acc[:, 0:128], gt_ref[:, 0:128]), cos, sin) * (TA_DIM ** -0.5 * LOG2E)
    tk = _rope(nrm(acc[:, 128:256], gt_ref[:, 128:256]), cos, sin)
    tv = acc[:, 256:384]
    lane = lax.broadcasted_iota(jnp.int32, tv.shape, 1)
    c0 = ZT_TA * IN_TILE
    z_ref[rs, c0:c0 + 128] = tq.astype(BF16)
    z_ref[rs, c0 + 128:c0 + 256] = tk.astype(BF16)
    z_ref[rs, c0 + 256:c0 + 384] = jnp.where(lane == TA_DIM, 1.0, tv).astype(BF16)
    z_ref[rs, c0 + 384:c0 + 512] = acc[:, 384:512].astype(BF16)

    put(0, head_rms(proj(0), gq_ref[...]) * (HEAD_DIM ** -0.5 * LOG2E))
    put(1, head_rms(proj(1), gk_ref[...]))
    put(ZT_SG_U, jax.nn.gelu(proj(3)))
    put(ZT_SG_V, jax.nn.gelu(proj(4)))
    put(ZT_CONV, proj(6) * jax.nn.sigmoid(proj(7)))
    put(2, proj(2))
    put(ZT_POOL, proj(5))


def _inproj(x2d, sh, sc, w_r, smat, gq, gk, gt, cos_t, sin_t, *, tpb):
    t, d = x2d.shape
    tm = min(512, tpb)
    tiles_pb = tpb // tm
    const = lambda i: (0, 0)
    return pl.pallas_call(
        _inproj_kernel,
        grid=(t // tm,),
        in_specs=[
            pl.BlockSpec((tm, d), lambda i: (i, 0)),
            pl.BlockSpec((1, 1, d), lambda i: (i // tiles_pb, 0, 0)),
            pl.BlockSpec((1, 1, d), lambda i: (i // tiles_pb, 0, 0)),
            pl.BlockSpec((d, N_IN_TILES * IN_TILE), const, pipeline_mode=pl.Buffered(1)),
            pl.BlockSpec((IN_TILE, IN_TILE), const),
            pl.BlockSpec((1, IN_TILE), const),
            pl.BlockSpec((1, IN_TILE), const),
            pl.BlockSpec((1, 256), const),
            pl.BlockSpec((tm, 128), lambda i: (i % tiles_pb, 0)),
            pl.BlockSpec((tm, 128), lambda i: (i % tiles_pb, 0)),
        ],
        out_specs=pl.BlockSpec((tm, Z_COLS), lambda i: (i, 0)),
        out_shape=jax.ShapeDtypeStruct((t, Z_COLS), BF16),
        scratch_shapes=[pltpu.VMEM((tm, d), BF16)],
        compiler_params=_cparams(("parallel",)),
        name="in_proj",
    )(x2d, sh, sc, w_r, smat, gq, gk, gt, cos_t, sin_t)


def _stack_heads(q):
    lane = lax.broadcasted_iota(jnp.int32, q.shape, 1)
    zero = jnp.zeros_like(q)
    return jnp.concatenate([jnp.where(lane < HEAD_DIM, q, zero), jnp.where(lane >= HEAD_DIM, q, zero)], axis=0)


def _unstack_heads(o2):
    m = o2.shape[0] // 2
    lane = lax.broadcasted_iota(jnp.int32, (m, 128), 1)
    return jnp.where(lane < HEAD_DIM, o2[:m], o2[m:])


def _na_x_kernel(q_ref, k_ref, v_ref, kc_ref, vc_ref, b_ref, o_ref, *, rows):
    tq = NA_QROWS * GRID_W
    nkeys = NA_BAND * GRID_W
    nrb = rows // NA_QROWS
    kc = kc_ref[...]
    vc = vc_ref[...]
    for sub in range(NA_SUB):
        rb = pl.program_id(2) * NA_SUB + sub
        rs = jnp.clip(rb * NA_QROWS - NA_WIN_R // 2, 0, rows - NA_BAND)
        start = pl.multiple_of(rs * GRID_W, GRID_W)
        case = jnp.where(rb == 0, 0, jnp.where(rb == nrb - 1, 2, 1))
        kw = k_ref[pl.ds(start, nkeys), :]
        vw = v_ref[pl.ds(start, nkeys), :]
        q2 = _stack_heads(q_ref[sub * tq:(sub + 1) * tq, :])
        s1 = _dot_nt(q2, kw) + b_ref[case, 0]
        s2 = _dot_nt(q2, kc)
        m = jnp.maximum(jnp.max(s1, axis=-1, keepdims=True), jnp.max(s2, axis=-1, keepdims=True))
        p1 = jnp.exp2(s1 - m)
        p2 = jnp.exp2(s2 - m)
        l = jnp.sum(p1, axis=-1, keepdims=True) + jnp.sum(p2, axis=-1, keepdims=True)
        o2 = _dot(p1.astype(BF16), vw) + _dot(p2.astype(BF16), vc)
        o_ref[sub * tq:(sub + 1) * tq, :] = _unstack_heads(o2 / l).astype(BF16)


def _na_x(z, zc, bias, *, nb, seq, lc):
    rows = seq // GRID_W
    assert rows % (NA_QROWS * NA_SUB) == 0 and rows >= NA_BAND
    nsteps = rows // (NA_QROWS * NA_SUB)
    tq = NA_QROWS * GRID_W
    return pl.pallas_call(
        functools.partial(_na_x_kernel, rows=rows),
        grid=(nb, NA_HEADS // 2, nsteps),
        in_specs=[
            pl.BlockSpec((NA_SUB * tq, 128), lambda b, hp, st: (b * nsteps + st, ZB_NA_Q + hp)),
            pl.BlockSpec((seq, 128), lambda b, hp, st: (b, ZB_NA_K + hp)),
            pl.BlockSpec((seq, 128), lambda b, hp, st: (b, ZB_NA_V + hp)),
            pl.BlockSpec((lc, 128), lambda b, hp, st: (b, ZB_NA_K + hp)),
            pl.BlockSpec((lc, 128), lambda b, hp, st: (b, ZB_NA_V + hp)),
            pl.BlockSpec((3, 1, 2 * tq, NA_BAND * GRID_W), lambda b, hp, st: (0, hp, 0, 0)),
        ],
        out_specs=pl.BlockSpec((NA_SUB * tq, 128), lambda b, hp, st: (b * nsteps + st, hp)),
        out_shape=jax.ShapeDtypeStruct((nb * seq, W_GRP), BF16),
        compiler_params=_cparams(("parallel", "parallel", "arbitrary")),
        name="na_attn",
    )(z, z, z, zc, zc, bias)


def _na_bias_table(rpb):
    h = rpb.shape[0]
    nq = NA_QROWS * GRID_W
    nk = NA_BAND * GRID_W
    qc, kc = np.arange(GRID_W), np.arange(GRID_W)
    c0 = np.clip(qc - NA_WIN_C // 2, 0, GRID_W - NA_WIN_C)
    col_ok = (kc[None, :] >= c0[:, None]) & (kc[None, :] < c0[:, None] + NA_WIN_C)
    cidx = np.clip(kc[None, :] - qc[:, None] + NA_WIN_C - 1, 0, 2 * NA_WIN_C - 2)
    col_sel = jnp.asarray((cidx.reshape(-1)[None, :] == np.arange(2 * NA_WIN_C - 1)[:, None]).astype(np.float32))
    qr, wr = np.arange(NA_QROWS), np.arange(NA_BAND)
    half = NA_WIN_R // 2
    cases = ((0, np.zeros(NA_QROWS, int)), (half, np.arange(NA_QROWS)), (NA_WIN_R, np.full(NA_QROWS, half)))
    tabs = []
    for dq, r0 in cases:
        row_ok = (wr[None, :] >= r0[:, None]) & (wr[None, :] < r0[:, None] + NA_WIN_R)
        ridx = np.clip(wr[None, :] - (dq + qr[:, None]) + NA_WIN_R - 1, 0, 2 * NA_WIN_R - 2)
        rows = rpb[:, ridx.reshape(-1), :]
        blocks = jnp.einsum("hrc,cx->hrx", rows, col_sel, precision=lax.Precision.HIGHEST)
        blocks = blocks.reshape(h, NA_QROWS, NA_BAND, GRID_W, GRID_W).transpose(0, 1, 3, 2, 4)
        valid = row_ok[:, None, :, None] & col_ok[None, :, None, :]
        tabs.append(jnp.where(jnp.asarray(valid)[None], blocks * LOG2E, NEG_BIG).reshape(h, nq, nk))
    tab = jnp.stack(tabs)
    return tab.reshape(3, h // 2, 2 * nq, nk).astype(F32)


def _na_ctx_kernel(q_ref, k_ref, v_ref, o_ref):
    q2 = _stack_heads(q_ref[...])
    s = _dot_nt(q2, k_ref[...])
    m = jnp.max(s, axis=-1, keepdims=True)
    p = jnp.exp2(s - m)
    l = jnp.sum(p, axis=-1, keepdims=True)
    o2 = _dot(p.astype(BF16), v_ref[...])
    o_ref[...] = _unstack_heads(o2 / l).astype(BF16)


def _na_ctx(zc, *, nb, lc):
    return pl.pallas_call(
        _na_ctx_kernel,
        grid=(nb, NA_HEADS // 2),
        in_specs=[
            pl.BlockSpec((lc, 128), lambda b, hp: (b, ZB_NA_Q + hp)),
            pl.BlockSpec((lc, 128), lambda b, hp: (b, ZB_NA_K + hp)),
            pl.BlockSpec((lc, 128), lambda b, hp: (b, ZB_NA_V + hp)),
        ],
        out_specs=pl.BlockSpec((lc, 128), lambda b, hp: (b, hp)),
        out_shape=jax.ShapeDtypeStruct((nb * lc, W_GRP), BF16),
        compiler_params=_cparams(("parallel", "parallel")),
        name="na_attn_ctx",
    )(zc, zc, zc)


def _ta_kernel(*refs, tk, nk, has_extra):
    if has_extra:
        q_ref, k_ref, v_ref, ke_ref, ve_ref, o_ref = refs
    else:
        q_ref, k_ref, v_ref, o_ref = refs
    q = q_ref[...]
    tq = q.shape[0]

    def step(kb, vb, carry):
        m, acc = carry
        s = _dot_nt(q, kb)
        mn = jnp.maximum(m, jnp.max(s, axis=-1, keepdims=True))
        p = jnp.exp2(s - mn).astype(BF16)
        acc = jnp.exp2(m - mn) * acc + _dot(p, vb)
        return mn, acc

    def body(c, carry):
        st = pl.multiple_of(c * tk, tk)
        return step(k_ref[pl.ds(st, tk), :], v_ref[pl.ds(st, tk), :], carry)

    carry = (jnp.full((tq, 1), NEG_BIG, F32), jnp.zeros((tq, 128), F32))
    carry = lax.fori_loop(0, nk, body, carry, unroll=8 if nk % 8 == 0 else 1)
    if has_extra:
        carry = step(ke_ref[...], ve_ref[...], carry)
    _, acc = carry
    lane = lax.broadcasted_iota(jnp.int32, acc.shape, 1)
    l = jnp.sum(jnp.where(lane == TA_DIM, acc, 0.0), axis=-1, keepdims=True)
    o_ref[...] = jnp.where(lane < TA_DIM, acc / l, 0.0).astype(BF16)


def _ta(zq, zextra, *, nb, lq, le):
    tq = min(512, lq)
    tk = min(1024, lq)
    nqb = lq // tq
    has_extra = zextra is not None
    in_specs = [
        pl.BlockSpec((tq, 128), lambda b, qi: (b * nqb + qi, ZB_TA_Q)),
        pl.BlockSpec((lq, 128), lambda b, qi: (b, ZB_TA_K)),
        pl.BlockSpec((lq, 128), lambda b, qi: (b, ZB_TA_V)),
    ]
    args = [zq, zq, zq]
    if has_extra:
        in_specs += [
            pl.BlockSpec((le, 128), lambda b, qi: (b, ZB_TA_K)),
            pl.BlockSpec((le, 128), lambda b, qi: (b, ZB_TA_V)),
        ]
        args += [zextra, zextra]
    return pl.pallas_call(
        functools.partial(_ta_kernel, tk=tk, nk=lq // tk, has_extra=has_extra),
        grid=(nb, nqb),
        in_specs=in_specs,
        out_specs=pl.BlockSpec((tq, 128), lambda b, qi: (b * nqb + qi, 0)),
        out_shape=jax.ShapeDtypeStruct((nb * lq, 128), BF16),
        compiler_params=_cparams(("parallel", "arbitrary")),
        name="ta_attn",
    )(*args)


def _sg_kernel(u_ref, gv_ref, tao_ref, gain_ref, wcat_ref, bsx_ref, two_ref, o_ref, v_scr):
    tm = u_ref.shape[0]
    gv = gv_ref[...].astype(F32)
    ms = jnp.mean(gv * gv, axis=-1, keepdims=True)
    v_scr[...] = (gv * lax.rsqrt(ms + EPS) * gain_ref[...]).astype(BF16)
    lane = lax.broadcasted_iota(jnp.int32, (SG_CHUNK, 128), 1)
    zero = jnp.zeros((SG_CHUNK, 128), BF16)

    def chunk(c, carry):
        r0 = pl.multiple_of(c * SG_CHUNK, SG_CHUNK)
        rows = pl.ds(r0, SG_CHUNK)
        outs = []
        for p in range(SG_HEADS // 2):
            vs = v_scr[rows, p * 128:(p + 1) * 128]
            rhs = jnp.concatenate([jnp.where(lane < 64, vs, zero), jnp.where(lane >= 64, vs, zero)], axis=0)
            outs.append(_dot(wcat_ref[p], rhs))
        s = jnp.concatenate(outs, axis=1) + bsx_ref[...] + _dot(tao_ref[rows, :], two_ref[...])
        o_ref[rows, :] = (u_ref[rows, :].astype(F32) * s).astype(BF16)
        return carry

    lax.fori_loop(0, tm // SG_CHUNK, chunk, 0)


def _sg(z, tao, gain, wcat, bsx, two, *, tpb):
    t = z.shape[0]
    tm = min(1024, tpb)
    const2 = lambda i: (0, 0)
    return pl.pallas_call(
        _sg_kernel,
        grid=(t // tm,),
        in_specs=[
            pl.BlockSpec((tm, W_GRP), lambda i: (i, ZT_SG_U)),
            pl.BlockSpec((tm, W_GRP), lambda i: (i, ZT_SG_V)),
            pl.BlockSpec((tm, 128), lambda i: (i, 0)),
            pl.BlockSpec((1, W_GRP), const2),
            pl.BlockSpec((SG_HEADS // 2, SG_CHUNK, 2 * SG_CHUNK), lambda i: (0, 0, 0)),
            pl.BlockSpec((SG_CHUNK, W_GRP), const2),
            pl.BlockSpec((128, W_GRP), const2),
        ],
        out_specs=pl.BlockSpec((tm, W_GRP), lambda i: (i, 0)),
        out_shape=jax.ShapeDtypeStruct((t, W_GRP), BF16),
        scratch_shapes=[pltpu.VMEM((tm, W_GRP), BF16)],
        compiler_params=_cparams(("parallel",)),
        name="sg_unit",
    )(z, z, tao, gain, wcat, bsx, two)


def _halo_specs(tm, t, width, col):
    hb = tm // HALO
    last = t // HALO - 1
    return [
        pl.BlockSpec((tm, width), lambda i: (i, col)),
        pl.BlockSpec((HALO, width), lambda i: (jnp.maximum(i * hb - 1, 0), col)),
        pl.BlockSpec((HALO, width), lambda i: (jnp.minimum((i + 1) * hb, last), col)),
    ]


def _fill_ext(ext_scr, main_ref, prev_ref, next_ref, it, tiles_pb):
    tm = main_ref.shape[0]
    zero = jnp.zeros(prev_ref.shape, F32)
    ext_scr[0:HALO, :] = jnp.where(it == 0, zero, prev_ref[...].astype(F32))
    ext_scr[HALO:HALO + tm, :] = main_ref[...].astype(F32)
    ext_scr[HALO + tm:2 * HALO + tm, :] = jnp.where(it == tiles_pb - 1, zero, next_ref[...].astype(F32))


def _pool_kernel(z_ref, zp_ref, zn_ref, wbd_ref, scale_ref, o_ref, ext_scr, d_scr, *, tiles_pb, seq):
    tm = z_ref.shape[0]
    it = pl.program_id(0) % tiles_pb
    _fill_ext(ext_scr, z_ref, zp_ref, zn_ref, it, tiles_pb)
    rc = 128

    for c in range(tm // rc):
        r0 = c * rc
        pos = it * tm + r0 + lax.broadcasted_iota(jnp.int32, (rc, 1), 0)
        for g, w in enumerate(PL_WINDOWS):
            cols = slice(g * 128, (g + 1) * 128)
            lo = r0 + HALO - w // 2
            acc = ext_scr[lo:lo + rc, cols]
            for s in range(1, w):
                acc = acc + ext_scr[lo + s:lo + s + rc, cols]
            cnt = (jnp.minimum(pos + w // 2, seq) - jnp.maximum(pos - w // 2, 0)).astype(F32)
            d = acc / cnt - ext_scr[r0 + HALO:r0 + HALO + rc, cols]
            d_scr[r0:r0 + rc, cols] = d.astype(BF16)
    o_ref[...] = (_dot(d_scr[...], wbd_ref[...]) * scale_ref[...]).astype(BF16)


def _pool(z, wbd, scale, *, tpb):
    t = z.shape[0]
    tm = min(1024, tpb)
    const2 = lambda i: (0, 0)
    return pl.pallas_call(
        functools.partial(_pool_kernel, tiles_pb=tpb // tm, seq=tpb),
        grid=(t // tm,),
        in_specs=_halo_specs(tm, t, W_GRP, ZT_POOL) + [
            pl.BlockSpec((W_GRP, W_GRP), const2),
            pl.BlockSpec((1, W_GRP), const2),
        ],
        out_specs=pl.BlockSpec((tm, W_GRP), lambda i: (i, 0)),
        out_shape=jax.ShapeDtypeStruct((t, W_GRP), BF16),
        scratch_shapes=[pltpu.VMEM((tm + 2 * HALO, W_GRP), F32), pltpu.VMEM((tm, W_GRP), BF16)],
        compiler_params=_cparams(("parallel",)),
        name="pool_mixer",
    )(z, z, z, wbd, scale)


def _conv_kernel(h_ref, hp_ref, hn_ref, dww_ref, dwb_ref, ng_ref, pww_ref, pwb_ref, o_ref, ext_scr, sh_scr,
                 a_scr, *, tiles_pb):
    tm = h_ref.shape[0]
    it = pl.program_id(0) % tiles_pb
    _fill_ext(ext_scr, h_ref, hp_ref, hn_ref, it, tiles_pb)
    rc = 64
    base = HALO - (CV_KERNEL - 1) // 2
    nsh = sh_scr.shape[1]
    for s in range(1, SUBLANES):
        sh_scr[s - 1] = ext_scr[s:s + nsh, :]

    for c in range(tm // rc):
        r0 = c * rc
        acc = jnp.broadcast_to(dwb_ref[...], (rc, W_GRP))
        for k in range(CV_KERNEL):
            a, s = divmod(base + k, SUBLANES)
            lo = r0 + a * SUBLANES
            rows = ext_scr[lo:lo + rc, :] if s == 0 else sh_scr[s - 1, lo:lo + rc, :]
            acc = acc + dww_ref[k:k + 1, :] * rows
        ms = jnp.mean(acc * acc, axis=-1, keepdims=True)
        hh = acc * lax.rsqrt(ms + EPS) * ng_ref[...]
        a_scr[r0:r0 + rc, :] = (hh * jax.nn.sigmoid(hh)).astype(BF16)
    o_ref[...] = (_dot(a_scr[...], pww_ref[...]) + pwb_ref[...]).astype(BF16)


def _conv(z, dww, dwb, ng, pww, pwb, *, tpb):
    t = z.shape[0]
    tm = min(512, tpb)
    const2 = lambda i: (0, 0)
    return pl.pallas_call(
        functools.partial(_conv_kernel, tiles_pb=tpb // tm),
        grid=(t // tm,),
        in_specs=_halo_specs(tm, t, W_GRP, ZT_CONV) + [
            pl.BlockSpec((CV_KERNEL, W_GRP), const2),
            pl.BlockSpec((1, W_GRP), const2),
            pl.BlockSpec((1, W_GRP), const2),
            pl.BlockSpec((W_GRP, W_GRP), const2),
            pl.BlockSpec((1, W_GRP), const2),
        ],
        out_specs=pl.BlockSpec((tm, W_GRP), lambda i: (i, 0)),
        out_shape=jax.ShapeDtypeStruct((t, W_GRP), BF16),
        scratch_shapes=[pltpu.VMEM((tm + 2 * HALO, W_GRP), F32),
                        pltpu.VMEM((SUBLANES - 1, tm + 2 * HALO - SUBLANES, W_GRP), F32),
                        pltpu.VMEM((tm, W_GRP), BF16)],
        compiler_params=_cparams(("parallel",)),
        name="conformer_conv",
    )(z, z, z, dww, dwb, ng, pww, pwb)


def _outproj_kernel(yna_ref, ysg_ref, ypl_ref, ycv_ref, w_ref, x_ref, g1_ref, sh2_ref, sc2_ref, x1_ref, hn_ref):
    tm, d = x_ref.shape
    y = jnp.concatenate([yna_ref[...], ysg_ref[...], ypl_ref[...], ycv_ref[...]], axis=1)
    nc = 512
    ssq = jnp.zeros((tm, 1), F32)
    for n in range(d // nc):
        cs = slice(n * nc, (n + 1) * nc)
        x1 = x_ref[:, cs] + g1_ref[0][:, cs] * _dot(y, w_ref[:, cs])
        x1_ref[:, cs] = x1
        ssq = ssq + jnp.sum(x1 * x1, axis=-1, keepdims=True)
    r = lax.rsqrt(ssq * (1.0 / d) + EPS)
    for n in range(d // nc):
        cs = slice(n * nc, (n + 1) * nc)
        hn_ref[:, cs] = (x1_ref[:, cs] * r * (1.0 + sc2_ref[0][:, cs]) + sh2_ref[0][:, cs]).astype(BF16)


def _outproj(ys, w_out, x2d, g1, sh2, sc2, *, tpb):
    t, d = x2d.shape
    tm = min(512, tpb)
    tiles_pb = tpb // tm
    row = lambda i: (i // tiles_pb, 0, 0)
    return pl.pallas_call(
        _outproj_kernel,
        grid=(t // tm,),
        in_specs=[pl.BlockSpec((tm, W_GRP), lambda i: (i, 0))] * 4 + [
            pl.BlockSpec((d, d), lambda i: (0, 0)),
            pl.BlockSpec((tm, d), lambda i: (i, 0)),
            pl.BlockSpec((1, 1, d), row),
            pl.BlockSpec((1, 1, d), row),
            pl.BlockSpec((1, 1, d), row),
        ],
        out_specs=[pl.BlockSpec((tm, d), lambda i: (i, 0)), pl.BlockSpec((tm, d), lambda i: (i, 0))],
        out_shape=[jax.ShapeDtypeStruct((t, d), F32), jax.ShapeDtypeStruct((t, d), BF16)],
        compiler_params=_cparams(("parallel",)),
        name="out_proj",
    )(*ys, w_out, x2d, g1, sh2, sc2)


def _ffn_kernel(hn_ref, hp_ref, hx_ref, x1_ref, g2_ref, wg_ref, wv_ref, wd_ref, dw_ref, o_ref, ext_scr, g_scr,
                *, tiles_pb, nx):
    tm, d = hn_ref.shape
    j = pl.program_id(1)
    it = pl.program_id(0) % tiles_pb

    @pl.when(j == 0)
    def _():
        zero = jnp.zeros((HALO, d), BF16)
        ext_scr[0:HALO, :] = jnp.where(it == 0, zero, hp_ref[...])
        ext_scr[HALO:HALO + tm, :] = hn_ref[...]
        ext_scr[HALO + tm:2 * HALO + tm, :] = jnp.where(it == tiles_pb - 1, zero, hx_ref[...])
        o_ref[...] = jnp.zeros(o_ref.shape, F32)

    g_scr[...] = _dot(ext_scr[...], wg_ref[...])
    val = _dot(hn_ref[...], wv_ref[...])
    c = (dw_ref[0:1, :] * g_scr[HALO - 1:HALO - 1 + tm, :] + dw_ref[1:2, :] * g_scr[HALO:HALO + tm, :]
         + dw_ref[2:3, :] * g_scr[HALO + 1:HALO + 1 + tm, :])
    a = (c * jax.nn.sigmoid(c) * val).astype(BF16)
    nc = 512
    for n in range(d // nc):
        cs = slice(n * nc, (n + 1) * nc)
        o_ref[:, cs] += g2_ref[0][:, cs] * _dot(a, wd_ref[:, cs])

    xc = d // nx
    for s in range(nx):
        @pl.when(j == s)
        def _(s=s):
            o_ref[:, s * xc:(s + 1) * xc] += x1_ref[...]


def _ffn(hn, x1, g2, w_up, wd, dw, *, tpb):
    t, d = x1.shape
    ff = wd.shape[0]
    tm = min(1024, tpb)
    tiles_pb = tpb // tm
    tf = 512
    nj = ff // tf
    nx = 8
    assert nj >= nx
    hb = tm // HALO
    last = t // HALO - 1
    return pl.pallas_call(
        functools.partial(_ffn_kernel, tiles_pb=tiles_pb, nx=nx),
        grid=(t // tm, nj),
        in_specs=[
            pl.BlockSpec((tm, d), lambda i, j: (i, 0)),
            pl.BlockSpec((HALO, d), lambda i, j: (jnp.maximum(i * hb - 1, 0), 0)),
            pl.BlockSpec((HALO, d), lambda i, j: (jnp.minimum((i + 1) * hb, last), 0)),
            pl.BlockSpec((tm, d // nx), lambda i, j: (i, jnp.minimum(j, nx - 1))),
            pl.BlockSpec((1, 1, d), lambda i, j: (i // tiles_pb, 0, 0)),
            pl.BlockSpec((d, tf), lambda i, j: (0, j)),
            pl.BlockSpec((d, tf), lambda i, j: (0, nj + j)),
            pl.BlockSpec((tf, d), lambda i, j: (j, 0)),
            pl.BlockSpec((FF_KERNEL, tf), lambda i, j: (0, j)),
        ],
        out_specs=pl.BlockSpec((tm, d), lambda i, j: (i, 0)),
        out_shape=jax.ShapeDtypeStruct((t, d), F32),
        scratch_shapes=[pltpu.VMEM((tm + 2 * HALO, d), BF16), pltpu.VMEM((tm + 2 * HALO, tf), F32)],
        compiler_params=_cparams(("parallel", "arbitrary")),
        name="conv_glu_ffn",
    )(hn, hn, hn, x1, g2, w_up, w_up, wd, dw)


def _rope_tables(seq):
    half = TA_DIM // 2
    freqs = ROPE_BASE ** (-np.arange(0, half, 2, dtype=np.float32) / half)
    pos = np.arange(seq)
    lane = np.arange(TA_DIM)
    p = np.where(lane[None, :] < half, (pos // GRID_W)[:, None], (pos % GRID_W)[:, None]).astype(np.float32)
    m = lane % half
    ang = jnp.asarray(p * freqs[m % (half // 2)][None, :])
    sign = jnp.asarray(np.where(m < half // 2, -1.0, 1.0).astype(np.float32))[None, :]
    cos = jnp.concatenate([jnp.cos(ang), jnp.ones((seq, 128 - TA_DIM), F32)], axis=1)
    sin = jnp.concatenate([jnp.sin(ang) * sign, jnp.zeros((seq, 128 - TA_DIM), F32)], axis=1)
    return cos, sin


def _layout_w_in(w):
    d = w.shape[0]
    g = W_GRP
    na, sg = w[:, 0:3 * g], w[:, 3 * g:5 * g]
    o = 5 * g
    tq, tk, tv = w[:, o:o + TA_DIM], w[:, o + TA_DIM:o + 2 * TA_DIM], w[:, o + 2 * TA_DIM:o + 3 * TA_DIM]
    rest = w[:, o + 3 * TA_DIM:]
    z64 = jnp.zeros((d, 128 - TA_DIM), w.dtype)
    ta = jnp.concatenate([tq, z64, tk, z64, tv, jnp.zeros((d, IN_TILE - 3 * 128 + 128 - TA_DIM), w.dtype)], axis=1)
    return jnp.concatenate([na, sg, rest, ta], axis=1).astype(BF16)


def _layer_params(l, w_in, na_q_gain, na_k_gain, na_rpb, sg_v_gain, sg_ws, sg_bs, ta_q_gain, ta_k_gain,
                  ta_w_out, pl_w, pl_scale, cv_dw_w, cv_dw_b, cv_norm_gain, cv_pw_w, cv_pw_b, w_out,
                  ff_w_up, ff_dw_w, ff_w_down):
    ones64 = jnp.ones((128 - TA_DIM,), F32)
    ws = sg_ws[l].astype(BF16)
    ff = ff_w_down.shape[1]
    p = dict(
        w_in=_layout_w_in(w_in[l]),
        gq=jnp.tile(na_q_gain[l], NA_HEADS)[None, :],
        gk=jnp.tile(na_k_gain[l], NA_HEADS)[None, :],
        gt=jnp.concatenate([ta_q_gain[l], ones64, ta_k_gain[l], ones64])[None, :],
        na_bias=_na_bias_table(na_rpb[l]),
        sg_gain=sg_v_gain[l][None, :],
        sg_wcat=jnp.concatenate([ws[0::2], ws[1::2]], axis=2),
        sg_bsx=jnp.repeat(sg_bs[l].T, W_GRP // SG_HEADS, axis=1),
        ta_w_out=jnp.concatenate([ta_w_out[l], jnp.zeros((128 - TA_DIM, W_GRP), F32)], axis=0).astype(BF16),
        pl_wbd=jax.scipy.linalg.block_diag(*[pl_w[l, g] for g in range(len(PL_WINDOWS))]).astype(BF16),
        pl_scale=pl_scale[l][None, :],
        cv_dw_w=cv_dw_w[l], cv_dw_b=cv_dw_b[l][None, :], cv_ng=cv_norm_gain[l][None, :],
        cv_pw_w=cv_pw_w[l].astype(BF16), cv_pw_b=cv_pw_b[l][None, :],
        w_out=w_out[l].astype(BF16),
        ff_w_up=ff_w_up[l].astype(BF16), ff_dw=ff_dw_w[l], ff_wd=ff_w_down[l].astype(BF16),
    )
    return p


def _smat():
    idx = np.arange(IN_TILE) // HEAD_DIM
    return jnp.asarray((idx[:, None] == idx[None, :]).astype(np.float32)).astype(BF16)


def _mixers(p, z, na_o, ta_o, *, tpb):
    sg_o = _sg(z, ta_o, p["sg_gain"], p["sg_wcat"], p["sg_bsx"], p["ta_w_out"], tpb=tpb)
    pl_o = _pool(z, p["pl_wbd"], p["pl_scale"], tpb=tpb)
    cv_o = _conv(z, p["cv_dw_w"], p["cv_dw_b"], p["cv_ng"], p["cv_pw_w"], p["cv_pw_b"], tpb=tpb)
    return [na_o, sg_o, pl_o, cv_o]


def _tail(p, ys, x2d, mods, *, tpb):
    x1, hn = _outproj(ys, p["w_out"], x2d, mods[2], mods[3], mods[4], tpb=tpb)
    return _ffn(hn, x1, mods[5], p["ff_w_up"], p["ff_wd"], p["ff_dw"], tpb=tpb)


def kernel(x, c, ctx, c_ctx, w_mod, b_mod, w_in, na_q_gain, na_k_gain, na_rpb, sg_v_gain, sg_ws, sg_bs,
           ta_q_gain, ta_k_gain, ta_w_out, pl_w, pl_scale, cv_dw_w, cv_dw_b, cv_norm_gain, cv_pw_w, cv_pw_b,
           w_out, ff_w_up, ff_dw_w, ff_w_down):
    nb, seq, d = x.shape
    lc = ctx.shape[1]
    depth = w_mod.shape[0]
    mods_all = _mod_vectors(c, c_ctx, w_mod, b_mod).reshape(depth, 8, 6, d)
    smat = _smat()
    cos_x, sin_x = _rope_tables(seq)
    cos_c, sin_c = jnp.ones((lc, 128), F32), jnp.zeros((lc, 128), F32)

    x2d = x.reshape(nb * seq, d)
    xc2d = ctx.reshape(nb * lc, d)
    for l in range(depth):
        last = l == depth - 1
        p = _layer_params(l, w_in, na_q_gain, na_k_gain, na_rpb, sg_v_gain, sg_ws, sg_bs, ta_q_gain, ta_k_gain,
                          ta_w_out, pl_w, pl_scale, cv_dw_w, cv_dw_b, cv_norm_gain, cv_pw_w, cv_pw_b, w_out,
                          ff_w_up, ff_dw_w, ff_w_down)
        mx = [mods_all[l, :nb, k][:, None, :] for k in range(6)]
        mc = [jnp.broadcast_to(mods_all[l, nb, k][None, None, :], (nb, 1, d)) for k in range(6)]
        zc = _inproj(xc2d, mc[0], mc[1], p["w_in"], smat, p["gq"], p["gk"], p["gt"], cos_c, sin_c, tpb=lc)
        z = _inproj(x2d, mx[0], mx[1], p["w_in"], smat, p["gq"], p["gk"], p["gt"], cos_x, sin_x, tpb=seq)
        na_o = _na_x(z, zc, p["na_bias"], nb=nb, seq=seq, lc=lc)
        ta_o = _ta(z, zc, nb=nb, lq=seq, le=lc)
        ys = _mixers(p, z, na_o, ta_o, tpb=seq)
        if not last:
            na_oc = _na_ctx(zc, nb=nb, lc=lc)
            ta_oc = _ta(zc, None, nb=nb, lq=lc, le=0)
            ysc = _mixers(p, zc, na_oc, ta_oc, tpb=lc)
            xc2d = _tail(p, ysc, xc2d, mc, tpb=lc)
        x2d = _tail(p, ys, x2d, mx, tpb=seq)
    return x2d.reshape(nb, seq, d)
```

```python
import functools

import numpy as np
import jax
import jax.numpy as jnp
from jax import lax
from jax.experimental import pallas as pl
from jax.experimental.pallas import tpu as pltpu

F32 = jnp.float32
BF16 = jnp.bfloat16

D_MODEL = 2048
GRID_W = 64
HEAD_DIM = 64
W_GRP = D_MODEL // 4
NA_HEADS = W_GRP // HEAD_DIM
NA_WIN_R = 8
NA_WIN_C = 16
NA_QROWS = 4
NA_BAND = NA_WIN_R + NA_QROWS
NA_SUB = 8
LOG2E = 1.4426950408889634
SG_HEADS = 8
SG_CHUNK = 128
TA_DIM = 64
PL_WINDOWS = (2, 4, 8, 16)
CV_KERNEL = 31
FF_KERNEL = 3
ROPE_BASE = 10000.0
EPS = 1e-6
HALO = 16
SUBLANES = 8
IN_TILE = 512
N_IN_TILES = 9
Z_COLS = 8 * IN_TILE
ZB_NA_Q, ZB_NA_K, ZB_NA_V = 0, 4, 8
ZB_TA_Q, ZB_TA_K, ZB_TA_V = 28, 29, 30
ZT_SG_U, ZT_SG_V, ZT_POOL, ZT_CONV, ZT_TA = 3, 4, 5, 6, 7
NEG_BIG = -1e30
NT_DIMS = (((1,), (1,)), ((), ()))
VMEM_LIMIT_MB = 56


def _cparams(sem, vmem_mb=VMEM_LIMIT_MB):
    return pltpu.CompilerParams(dimension_semantics=sem, vmem_limit_bytes=vmem_mb * 1024 * 1024)


def _dot(a, b):
    return jnp.dot(a, b, preferred_element_type=F32)


def _dot_nt(a, b):
    return lax.dot_general(a, b, NT_DIMS, preferred_element_type=F32)


def _mod_kernel(c_ref, w_ref, b_ref, o_ref):
    cv = c_ref[...]
    s = cv * jax.nn.sigmoid(cv)
    o_ref[0] = _dot(s.astype(BF16), w_ref[0].astype(BF16)) + b_ref[0]


def _mod_vectors(c, c_ctx, w_mod, b_mod):
    depth, d, n = w_mod.shape
    nb = c.shape[0]
    assert nb + 1 <= 8
    cv = jnp.zeros((8, d), F32).at[:nb].set(c).at[nb].set(c_ctx)
    tn = 1024
    return pl.pallas_call(
        _mod_kernel,
        grid=(depth, n // tn),
        in_specs=[
            pl.BlockSpec((8, d), lambda l, k: (0, 0)),
            pl.BlockSpec((1, d, tn), lambda l, k: (l, 0, k)),
            pl.BlockSpec((1, 1, tn), lambda l, k: (l, 0, k)),
        ],
        out_specs=pl.BlockSpec((1, 8, tn), lambda l, k: (l, 0, k)),
        out_shape=jax.ShapeDtypeStruct((depth, 8, n), F32),
        compiler_params=_cparams(("parallel", "parallel"), 40),
        name="mod_vectors",
    )(cv, w_mod, b_mod.reshape(depth, 1, n))


def _rope(t, cos, sin):
    lane = lax.broadcasted_iota(jnp.int32, t.shape, 1)
    partner = jnp.where((lane % 32) < 16, pltpu.roll(t, 128 - 16, 1), pltpu.roll(t, 16, 1))
    return t * cos + partner * sin


def _inproj_kernel(x_ref, sh_ref, sc_ref, w_ref, smat_ref, gq_ref, gk_ref, gt_ref, cos_ref, sin_ref,
                   z_ref, xn_scr):
    half = x_ref.shape[0] // 2
    for r in range(2):
        _inproj_rows(slice(r * half, (r + 1) * half), x_ref, sh_ref, sc_ref, w_ref, smat_ref, gq_ref, gk_ref,
                     gt_ref, cos_ref, sin_ref, z_ref, xn_scr)


def _inproj_rows(rs, x_ref, sh_ref, sc_ref, w_ref, smat_ref, gq_ref, gk_ref, gt_ref, cos_ref, sin_ref,
                 z_ref, xn_scr):
    x = x_ref[rs, :]
    ms = jnp.mean(x * x, axis=-1, keepdims=True)
    xn_scr[rs, :] = (x * lax.rsqrt(ms + EPS) * (1.0 + sc_ref[0]) + sh_ref[0]).astype(BF16)

    def proj(j):
        return _dot(xn_scr[rs, :], w_ref[:, j * IN_TILE:(j + 1) * IN_TILE])

    def put(t, val):
        z_ref[rs, t * IN_TILE:(t + 1) * IN_TILE] = val.astype(BF16)

    def head_rms(a, gain):
        sq = a * a
        hi = sq.astype(BF16)
        lo = (sq - hi.astype(F32)).astype(BF16)
        ss = _dot(hi, smat_ref[...]) + _dot(lo, smat_ref[...])
        return a * lax.rsqrt(ss * (1.0 / HEAD_DIM) + EPS) * gain

    acc = proj(8)

    def nrm(t, g):
        ms_t = jnp.sum(t * t, axis=-1, keepdims=True) * (1.0 / TA_DIM)
        return t * lax.rsqrt(ms_t + EPS) * g
    cos = cos_ref[rs, :]
    sin = sin_ref[rs, :]
    tq = _rope(nrm(acc[:, 0:128], gt_ref[:, 0:128]), cos, sin) * (TA_DIM ** -0.5 * LOG2E)
    tk = _rope(nrm(acc[:, 128:256], gt_ref[:, 128:256]), cos, sin)
    tv = acc[:, 256:384]
    lane = lax.broadcasted_iota(jnp.int32, tv.shape, 1)
    c0 = ZT_TA * IN_TILE
    z_ref[rs, c0:c0 + 128] = tq.astype(BF16)
    z_ref[rs, c0 + 128:c0 + 256] = tk.astype(BF16)
    z_ref[rs, c0 + 256:c0 + 384] = jnp.where(lane == TA_DIM, 1.0, tv).astype(BF16)
    z_ref[rs, c0 + 384:c0 + 512] = acc[:, 384:512].astype(BF16)

    put(0, head_rms(proj(0), gq_ref[...]) * (HEAD_DIM ** -0.5 * LOG2E))
    put(1, head_rms(proj(1), gk_ref[...]))
    put(ZT_SG_U, jax.nn.gelu(proj(3)))
    put(ZT_SG_V, jax.nn.gelu(proj(4)))
    put(ZT_CONV, proj(6) * jax.nn.sigmoid(proj(7)))
    put(2, proj(2))
    put(ZT_POOL, proj(5))


def _inproj(x2d, sh, sc, w_r, smat, gq, gk, gt, cos_t, sin_t, *, tpb):
    t, d = x2d.shape
    tm = min(512, tpb)
    tiles_pb = tpb // tm
    const = lambda i: (0, 0)
    return pl.pallas_call(
        _inproj_kernel,
        grid=(t // tm,),
        in_specs=[
            pl.BlockSpec((tm, d), lambda i: (i, 0)),
            pl.BlockSpec((1, 1, d), lambda i: (i // tiles_pb, 0, 0)),
            pl.BlockSpec((1, 1, d), lambda i: (i // tiles_pb, 0, 0)),
            pl.BlockSpec((d, N_IN_TILES * IN_TILE), const, pipeline_mode=pl.Buffered(1)),
            pl.BlockSpec((IN_TILE, IN_TILE), const),
            pl.BlockSpec((1, IN_TILE), const),
            pl.BlockSpec((1, IN_TILE), const),
            pl.BlockSpec((1, 256), const),
            pl.BlockSpec((tm, 128), lambda i: (i % tiles_pb, 0)),
            pl.BlockSpec((tm, 128), lambda i: (i % tiles_pb, 0)),
        ],
        out_specs=pl.BlockSpec((tm, Z_COLS), lambda i: (i, 0)),
        out_shape=jax.ShapeDtypeStruct((t, Z_COLS), BF16),
        scratch_shapes=[pltpu.VMEM((tm, d), BF16)],
        compiler_params=_cparams(("parallel",)),
        name="in_proj",
    )(x2d, sh, sc, w_r, smat, gq, gk, gt, cos_t, sin_t)


def _stack_heads(q):
    lane = lax.broadcasted_iota(jnp.int32, q.shape, 1)
    zero = jnp.zeros_like(q)
    return jnp.concatenate([jnp.where(lane < HEAD_DIM, q, zero), jnp.where(lane >= HEAD_DIM, q, zero)], axis=0)


def _unstack_heads(o2):
    m = o2.shape[0] // 2
    lane = lax.broadcasted_iota(jnp.int32, (m, 128), 1)
    return jnp.where(lane < HEAD_DIM, o2[:m], o2[m:])


def _na_x_kernel(q_ref, k_ref, v_ref, kc_ref, vc_ref, b_ref, o_ref, *, rows):
    tq = NA_QROWS * GRID_W
    nkeys = NA_BAND * GRID_W
    nrb = rows // NA_QROWS
    kc = kc_ref[...]
    vc = vc_ref[...]
    for sub in range(NA_SUB):
        rb = pl.program_id(2) * NA_SUB + sub
        rs = jnp.clip(rb * NA_QROWS - NA_WIN_R // 2, 0, rows - NA_BAND)
        start = pl.multiple_of(rs * GRID_W, GRID_W)
        case = jnp.where(rb == 0, 0, jnp.where(rb == nrb - 1, 2, 1))
        kw = k_ref[pl.ds(start, nkeys), :]
        vw = v_ref[pl.ds(start, nkeys), :]
        q2 = _stack_heads(q_ref[sub * tq:(sub + 1) * tq, :])
        s1 = _dot_nt(q2, kw) + b_ref[case, 0]
        s2 = _dot_nt(q2, kc)
        m = jnp.maximum(jnp.max(s1, axis=-1, keepdims=True), jnp.max(s2, axis=-1, keepdims=True))
        p1 = jnp.exp2(s1 - m)
        p2 = jnp.exp2(s2 - m)
        l = jnp.sum(p1, axis=-1, keepdims=True) + jnp.sum(p2, axis=-1, keepdims=True)
        o2 = _dot(p1.astype(BF16), vw) + _dot(p2.astype(BF16), vc)
        o_ref[sub * tq:(sub + 1) * tq, :] = _unstack_heads(o2 / l).astype(BF16)


def _na_x(z, zc, bias, *, nb, seq, lc):
    rows = seq // GRID_W
    assert rows % (NA_QROWS * NA_SUB) == 0 and rows >= NA_BAND
    nsteps = rows // (NA_QROWS * NA_SUB)
    tq = NA_QROWS * GRID_W
    return pl.pallas_call(
        functools.partial(_na_x_kernel, rows=rows),
        grid=(nb, NA_HEADS // 2, nsteps),
        in_specs=[
            pl.BlockSpec((NA_SUB * tq, 128), lambda b, hp, st: (b * nsteps + st, ZB_NA_Q + hp)),
            pl.BlockSpec((seq, 128), lambda b, hp, st: (b, ZB_NA_K + hp)),
            pl.BlockSpec((seq, 128), lambda b, hp, st: (b, ZB_NA_V + hp)),
            pl.BlockSpec((lc, 128), lambda b, hp, st: (b, ZB_NA_K + hp)),
            pl.BlockSpec((lc, 128), lambda b, hp, st: (b, ZB_NA_V + hp)),
            pl.BlockSpec((3, 1, 2 * tq, NA_BAND * GRID_W), lambda b, hp, st: (0, hp, 0, 0)),
        ],
        out_specs=pl.BlockSpec((NA_SUB * tq, 128), lambda b, hp, st: (b * nsteps + st, hp)),
        out_shape=jax.ShapeDtypeStruct((nb * seq, W_GRP), BF16),
        compiler_params=_cparams(("parallel", "parallel", "arbitrary")),
        name="na_attn",
    )(z, z, z, zc, zc, bias)


def _na_bias_table(rpb):
    h = rpb.shape[0]
    nq = NA_QROWS * GRID_W
    nk = NA_BAND * GRID_W
    qc, kc = np.arange(GRID_W), np.arange(GRID_W)
    c0 = np.clip(qc - NA_WIN_C // 2, 0, GRID_W - NA_WIN_C)
    col_ok = (kc[None, :] >= c0[:, None]) & (kc[None, :] < c0[:, None] + NA_WIN_C)
    cidx = np.clip(kc[None, :] - qc[:, None] + NA_WIN_C - 1, 0, 2 * NA_WIN_C - 2)
    col_sel = jnp.asarray((cidx.reshape(-1)[None, :] == np.arange(2 * NA_WIN_C - 1)[:, None]).astype(np.float32))
    qr, wr = np.arange(NA_QROWS), np.arange(NA_BAND)
    half = NA_WIN_R // 2
    cases = ((0, np.zeros(NA_QROWS, int)), (half, np.arange(NA_QROWS)), (NA_WIN_R, np.full(NA_QROWS, half)))
    ridx, valid = [], []
    for dq, r0 in cases:
        row_ok = (wr[None, :] >= r0[:, None]) & (wr[None, :] < r0[:, None] + NA_WIN_R)
        ridx.append(np.clip(wr[None, :] - (dq + qr[:, None]) + NA_WIN_R - 1, 0, 2 * NA_WIN_R - 2).reshape(-1))
        valid.append(row_ok[:, None, :, None] & col_ok[None, :, None, :])
    rows = rpb[:, np.stack(ridx), :] * LOG2E
    blocks = jnp.einsum("hbrc,cx->bhrx", rows, col_sel, precision=lax.Precision.HIGHEST)
    blocks = blocks.reshape(3, h, NA_QROWS, NA_BAND, GRID_W, GRID_W).transpose(0, 1, 2, 4, 3, 5)
    tab = jnp.where(jnp.asarray(np.stack(valid))[:, None], blocks, NEG_BIG)
    return tab.reshape(3, h // 2, 2 * nq, nk)


def _na_ctx_kernel(q_ref, k_ref, v_ref, o_ref):
    q2 = _stack_heads(q_ref[...])
    s = _dot_nt(q2, k_ref[...])
    m = jnp.max(s, axis=-1, keepdims=True)
    p = jnp.exp2(s - m)
    l = jnp.sum(p, axis=-1, keepdims=True)
    o2 = _dot(p.astype(BF16), v_ref[...])
    o_ref[...] = _unstack_heads(o2 / l).astype(BF16)


def _na_ctx(zc, *, nb, lc):
    return pl.pallas_call(
        _na_ctx_kernel,
        grid=(nb, NA_HEADS // 2),
        in_specs=[
            pl.BlockSpec((lc, 128), lambda b, hp: (b, ZB_NA_Q + hp)),
            pl.BlockSpec((lc, 128), lambda b, hp: (b, ZB_NA_K + hp)),
            pl.BlockSpec((lc, 128), lambda b, hp: (b, ZB_NA_V + hp)),
        ],
        out_specs=pl.BlockSpec((lc, 128), lambda b, hp: (b, hp)),
        out_shape=jax.ShapeDtypeStruct((nb * lc, W_GRP), BF16),
        compiler_params=_cparams(("parallel", "parallel")),
        name="na_attn_ctx",
    )(zc, zc, zc)


def _ta_kernel(*refs, tk, nk, has_extra):
    if has_extra:
        q_ref, k_ref, v_ref, ke_ref, ve_ref, o_ref = refs
    else:
        q_ref, k_ref, v_ref, o_ref = refs
    q = q_ref[...]
    tq = q.shape[0]

    def step(kb, vb, carry):
        m, acc = carry
        s = _dot_nt(q, kb)
        mn = jnp.maximum(m, jnp.max(s, axis=-1, keepdims=True))
        p = jnp.exp2(s - mn).astype(BF16)
        acc = jnp.exp2(m - mn) * acc + _dot(p, vb)
        return mn, acc

    def body(c, carry):
        st = pl.multiple_of(c * tk, tk)
        return step(k_ref[pl.ds(st, tk), :], v_ref[pl.ds(st, tk), :], carry)

    carry = (jnp.full((tq, 1), NEG_BIG, F32), jnp.zeros((tq, 128), F32))
    carry = lax.fori_loop(0, nk, body, carry, unroll=8 if nk % 8 == 0 else 1)
    if has_extra:
        carry = step(ke_ref[...], ve_ref[...], carry)
    _, acc = carry
    lane = lax.broadcasted_iota(jnp.int32, acc.shape, 1)
    l = jnp.sum(jnp.where(lane == TA_DIM, acc, 0.0), axis=-1, keepdims=True)
    o_ref[...] = jnp.where(lane < TA_DIM, acc / l, 0.0).astype(BF16)


def _ta(zq, zextra, *, nb, lq, le):
    tq = min(1024, lq)
    tk = min(1024, lq)
    nqb = lq // tq
    has_extra = zextra is not None
    in_specs = [
        pl.BlockSpec((tq, 128), lambda b, qi: (b * nqb + qi, ZB_TA_Q)),
        pl.BlockSpec((lq, 128), lambda b, qi: (b, ZB_TA_K)),
        pl.BlockSpec((lq, 128), lambda b, qi: (b, ZB_TA_V)),
    ]
    args = [zq, zq, zq]
    if has_extra:
        in_specs += [
            pl.BlockSpec((le, 128), lambda b, qi: (b, ZB_TA_K)),
            pl.BlockSpec((le, 128), lambda b, qi: (b, ZB_TA_V)),
        ]
        args += [zextra, zextra]
    return pl.pallas_call(
        functools.partial(_ta_kernel, tk=tk, nk=lq // tk, has_extra=has_extra),
        grid=(nb, nqb),
        in_specs=in_specs,
        out_specs=pl.BlockSpec((tq, 128), lambda b, qi: (b * nqb + qi, 0)),
        out_shape=jax.ShapeDtypeStruct((nb * lq, 128), BF16),
        compiler_params=_cparams(("parallel", "arbitrary")),
        name="ta_attn",
    )(*args)


def _sg_kernel(u_ref, gv_ref, tao_ref, gain_ref, wcat_ref, bsx_ref, two_ref, o_ref, v_scr):
    tm = u_ref.shape[0]
    gv = gv_ref[...].astype(F32)
    ms = jnp.mean(gv * gv, axis=-1, keepdims=True)
    v_scr[...] = (gv * lax.rsqrt(ms + EPS) * gain_ref[...]).astype(BF16)
    lane = lax.broadcasted_iota(jnp.int32, (SG_CHUNK, 128), 1)
    zero = jnp.zeros((SG_CHUNK, 128), BF16)

    def chunk(c, carry):
        r0 = pl.multiple_of(c * SG_CHUNK, SG_CHUNK)
        rows = pl.ds(r0, SG_CHUNK)
        outs = []
        for p in range(SG_HEADS // 2):
            vs = v_scr[rows, p * 128:(p + 1) * 128]
            rhs = jnp.concatenate([jnp.where(lane < 64, vs, zero), jnp.where(lane >= 64, vs, zero)], axis=0)
            outs.append(_dot(wcat_ref[p], rhs))
        s = jnp.concatenate(outs, axis=1) + bsx_ref[...] + _dot(tao_ref[rows, :], two_ref[...])
        o_ref[rows, :] = (u_ref[rows, :].astype(F32) * s).astype(BF16)
        return carry

    lax.fori_loop(0, tm // SG_CHUNK, chunk, 0)


def _sg(z, tao, gain, wcat, bsx, two, *, tpb):
    t = z.shape[0]
    tm = min(1024, tpb)
    const2 = lambda i: (0, 0)
    return pl.pallas_call(
        _sg_kernel,
        grid=(t // tm,),
        in_specs=[
            pl.BlockSpec((tm, W_GRP), lambda i: (i, ZT_SG_U)),
            pl.BlockSpec((tm, W_GRP), lambda i: (i, ZT_SG_V)),
            pl.BlockSpec((tm, 128), lambda i: (i, 0)),
            pl.BlockSpec((1, W_GRP), const2),
            pl.BlockSpec((SG_HEADS // 2, SG_CHUNK, 2 * SG_CHUNK), lambda i: (0, 0, 0)),
            pl.BlockSpec((SG_CHUNK, W_GRP), const2),
            pl.BlockSpec((128, W_GRP), const2),
        ],
        out_specs=pl.BlockSpec((tm, W_GRP), lambda i: (i, 0)),
        out_shape=jax.ShapeDtypeStruct((t, W_GRP), BF16),
        scratch_shapes=[pltpu.VMEM((tm, W_GRP), BF16)],
        compiler_params=_cparams(("parallel",)),
        name="sg_unit",
    )(z, z, tao, gain, wcat, bsx, two)


def _halo_specs(tm, t, width, col):
    hb = tm // HALO
    last = t // HALO - 1
    return [
        pl.BlockSpec((tm, width), lambda i: (i, col)),
        pl.BlockSpec((HALO, width), lambda i: (jnp.maximum(i * hb - 1, 0), col)),
        pl.BlockSpec((HALO, width), lambda i: (jnp.minimum((i + 1) * hb, last), col)),
    ]


def _fill_ext(ext_scr, main_ref, prev_ref, next_ref, it, tiles_pb):
    tm = main_ref.shape[0]
    zero = jnp.zeros(prev_ref.shape, F32)
    ext_scr[0:HALO, :] = jnp.where(it == 0, zero, prev_ref[...].astype(F32))
    ext_scr[HALO:HALO + tm, :] = main_ref[...].astype(F32)
    ext_scr[HALO + tm:2 * HALO + tm, :] = jnp.where(it == tiles_pb - 1, zero, next_ref[...].astype(F32))


def _pool_kernel(z_ref, zp_ref, zn_ref, wbd_ref, scale_ref, o_ref, ext_scr, d_scr, *, tiles_pb, seq):
    tm = z_ref.shape[0]
    it = pl.program_id(0) % tiles_pb
    _fill_ext(ext_scr, z_ref, zp_ref, zn_ref, it, tiles_pb)
    rc = 128

    for c in range(tm // rc):
        r0 = c * rc
        pos = it * tm + r0 + lax.broadcasted_iota(jnp.int32, (rc, 1), 0)
        for g, w in enumerate(PL_WINDOWS):
            cols = slice(g * 128, (g + 1) * 128)
            lo = r0 + HALO - w // 2
            acc = ext_scr[lo:lo + rc, cols]
            for s in range(1, w):
                acc = acc + ext_scr[lo + s:lo + s + rc, cols]
            cnt = (jnp.minimum(pos + w // 2, seq) - jnp.maximum(pos - w // 2, 0)).astype(F32)
            d = acc / cnt - ext_scr[r0 + HALO:r0 + HALO + rc, cols]
            d_scr[r0:r0 + rc, cols] = d.astype(BF16)
    o_ref[...] = (_dot(d_scr[...], wbd_ref[...]) * scale_ref[...]).astype(BF16)


def _pool(z, wbd, scale, *, tpb):
    t = z.shape[0]
    tm = min(1024, tpb)
    const2 = lambda i: (0, 0)
    return pl.pallas_call(
        functools.partial(_pool_kernel, tiles_pb=tpb // tm, seq=tpb),
        grid=(t // tm,),
        in_specs=_halo_specs(tm, t, W_GRP, ZT_POOL) + [
            pl.BlockSpec((W_GRP, W_GRP), const2),
            pl.BlockSpec((1, W_GRP), const2),
        ],
        out_specs=pl.BlockSpec((tm, W_GRP), lambda i: (i, 0)),
        out_shape=jax.ShapeDtypeStruct((t, W_GRP), BF16),
        scratch_shapes=[pltpu.VMEM((tm + 2 * HALO, W_GRP), F32), pltpu.VMEM((tm, W_GRP), BF16)],
        compiler_params=_cparams(("parallel",)),
        name="pool_mixer",
    )(z, z, z, wbd, scale)


def _conv_kernel(h_ref, hp_ref, hn_ref, dww_ref, dwb_ref, ng_ref, pww_ref, pwb_ref, o_ref, ext_scr, sh_scr,
                 a_scr, *, tiles_pb):
    tm = h_ref.shape[0]
    it = pl.program_id(0) % tiles_pb
    _fill_ext(ext_scr, h_ref, hp_ref, hn_ref, it, tiles_pb)
    rc = 64
    base = HALO - (CV_KERNEL - 1) // 2
    nsh = sh_scr.shape[1]
    for s in range(1, SUBLANES):
        sh_scr[s - 1] = ext_scr[s:s + nsh, :]

    for c in range(tm // rc):
        r0 = c * rc
        acc = jnp.broadcast_to(dwb_ref[...], (rc, W_GRP))
        for k in range(CV_KERNEL):
            a, s = divmod(base + k, SUBLANES)
            lo = r0 + a * SUBLANES
            rows = ext_scr[lo:lo + rc, :] if s == 0 else sh_scr[s - 1, lo:lo + rc, :]
            acc = acc + dww_ref[k:k + 1, :] * rows
        ms = jnp.mean(acc * acc, axis=-1, keepdims=True)
        hh = acc * lax.rsqrt(ms + EPS) * ng_ref[...]
        a_scr[r0:r0 + rc, :] = (hh * jax.nn.sigmoid(hh)).astype(BF16)
    o_ref[...] = (_dot(a_scr[...], pww_ref[...]) + pwb_ref[...]).astype(BF16)


def _conv(z, dww, dwb, ng, pww, pwb, *, tpb):
    t = z.shape[0]
    tm = min(512, tpb)
    const2 = lambda i: (0, 0)
    return pl.pallas_call(
        functools.partial(_conv_kernel, tiles_pb=tpb // tm),
        grid=(t // tm,),
        in_specs=_halo_specs(tm, t, W_GRP, ZT_CONV) + [
            pl.BlockSpec((CV_KERNEL, W_GRP), const2),
            pl.BlockSpec((1, W_GRP), const2),
            pl.BlockSpec((1, W_GRP), const2),
            pl.BlockSpec((W_GRP, W_GRP), const2),
            pl.BlockSpec((1, W_GRP), const2),
        ],
        out_specs=pl.BlockSpec((tm, W_GRP), lambda i: (i, 0)),
        out_shape=jax.ShapeDtypeStruct((t, W_GRP), BF16),
        scratch_shapes=[pltpu.VMEM((tm + 2 * HALO, W_GRP), F32),
                        pltpu.VMEM((SUBLANES - 1, tm + 2 * HALO - SUBLANES, W_GRP), F32),
                        pltpu.VMEM((tm, W_GRP), BF16)],
        compiler_params=_cparams(("parallel",)),
        name="conformer_conv",
    )(z, z, z, dww, dwb, ng, pww, pwb)


def _outproj_kernel(yna_ref, ysg_ref, ypl_ref, ycv_ref, w_ref, x_ref, g1_ref, sh2_ref, sc2_ref, x1_ref, hn_ref):
    tm, d = x_ref.shape
    y = jnp.concatenate([yna_ref[...], ysg_ref[...], ypl_ref[...], ycv_ref[...]], axis=1)
    nc = 512
    ssq = jnp.zeros((tm, 1), F32)
    for n in range(d // nc):
        cs = slice(n * nc, (n + 1) * nc)
        x1 = x_ref[:, cs] + g1_ref[0][:, cs] * _dot(y, w_ref[:, cs])
        x1_ref[:, cs] = x1
        ssq = ssq + jnp.sum(x1 * x1, axis=-1, keepdims=True)
    r = lax.rsqrt(ssq * (1.0 / d) + EPS)
    for n in range(d // nc):
        cs = slice(n * nc, (n + 1) * nc)
        hn_ref[:, cs] = (x1_ref[:, cs] * r * (1.0 + sc2_ref[0][:, cs]) + sh2_ref[0][:, cs]).astype(BF16)


def _outproj(ys, w_out, x2d, g1, sh2, sc2, *, tpb):
    t, d = x2d.shape
    tm = min(512, tpb)
    tiles_pb = tpb // tm
    row = lambda i: (i // tiles_pb, 0, 0)
    return pl.pallas_call(
        _outproj_kernel,
        grid=(t // tm,),
        in_specs=[pl.BlockSpec((tm, W_GRP), lambda i: (i, 0))] * 4 + [
            pl.BlockSpec((d, d), lambda i: (0, 0)),
            pl.BlockSpec((tm, d), lambda i: (i, 0)),
            pl.BlockSpec((1, 1, d), row),
            pl.BlockSpec((1, 1, d), row),
            pl.BlockSpec((1, 1, d), row),
        ],
        out_specs=[pl.BlockSpec((tm, d), lambda i: (i, 0)), pl.BlockSpec((tm, d), lambda i: (i, 0))],
        out_shape=[jax.ShapeDtypeStruct((t, d), F32), jax.ShapeDtypeStruct((t, d), BF16)],
        compiler_params=_cparams(("parallel",)),
        name="out_proj",
    )(*ys, w_out, x2d, g1, sh2, sc2)


def _ffn_kernel(hn_ref, hp_ref, hx_ref, x1_ref, g2_ref, wg_ref, wv_ref, wd_ref, dw_ref, o_ref, ext_scr, g_scr,
                *, tiles_pb, nx):
    tm, d = hn_ref.shape
    j = pl.program_id(1)
    it = pl.program_id(0) % tiles_pb

    @pl.when(j == 0)
    def _():
        zero = jnp.zeros((HALO, d), BF16)
        ext_scr[0:HALO, :] = jnp.where(it == 0, zero, hp_ref[...])
        ext_scr[HALO:HALO + tm, :] = hn_ref[...]
        ext_scr[HALO + tm:2 * HALO + tm, :] = jnp.where(it == tiles_pb - 1, zero, hx_ref[...])
        o_ref[...] = jnp.zeros(o_ref.shape, F32)

    g_scr[...] = _dot(ext_scr[...], wg_ref[...])
    val = _dot(hn_ref[...], wv_ref[...])
    c = (dw_ref[0:1, :] * g_scr[HALO - 1:HALO - 1 + tm, :] + dw_ref[1:2, :] * g_scr[HALO:HALO + tm, :]
         + dw_ref[2:3, :] * g_scr[HALO + 1:HALO + 1 + tm, :])
    a = (c * jax.nn.sigmoid(c) * val).astype(BF16)
    nc = 512
    for n in range(d // nc):
        cs = slice(n * nc, (n + 1) * nc)
        o_ref[:, cs] += g2_ref[0][:, cs] * _dot(a, wd_ref[:, cs])

    xc = d // nx
    for s in range(nx):
        @pl.when(j == s)
        def _(s=s):
            o_ref[:, s * xc:(s + 1) * xc] += x1_ref[...]


def _ffn(hn, x1, g2, w_up, wd, dw, *, tpb):
    t, d = x1.shape
    ff = wd.shape[0]
    tm = min(1024, tpb)
    tiles_pb = tpb // tm
    tf = 512
    nj = ff // tf
    nx = 8
    assert nj >= nx
    hb = tm // HALO
    last = t // HALO - 1
    return pl.pallas_call(
        functools.partial(_ffn_kernel, tiles_pb=tiles_pb, nx=nx),
        grid=(t // tm, nj),
        in_specs=[
            pl.BlockSpec((tm, d), lambda i, j: (i, 0)),
            pl.BlockSpec((HALO, d), lambda i, j: (jnp.maximum(i * hb - 1, 0), 0)),
            pl.BlockSpec((HALO, d), lambda i, j: (jnp.minimum((i + 1) * hb, last), 0)),
            pl.BlockSpec((tm, d // nx), lambda i, j: (i, jnp.minimum(j, nx - 1))),
            pl.BlockSpec((1, 1, d), lambda i, j: (i // tiles_pb, 0, 0)),
            pl.BlockSpec((d, tf), lambda i, j: (0, j)),
            pl.BlockSpec((d, tf), lambda i, j: (0, nj + j)),
            pl.BlockSpec((tf, d), lambda i, j: (j, 0)),
            pl.BlockSpec((FF_KERNEL, tf), lambda i, j: (0, j)),
        ],
        out_specs=pl.BlockSpec((tm, d), lambda i, j: (i, 0)),
        out_shape=jax.ShapeDtypeStruct((t, d), F32),
        scratch_shapes=[pltpu.VMEM((tm + 2 * HALO, d), BF16), pltpu.VMEM((tm + 2 * HALO, tf), F32)],
        compiler_params=_cparams(("parallel", "arbitrary")),
        name="conv_glu_ffn",
    )(hn, hn, hn, x1, g2, w_up, w_up, wd, dw)


def _rope_tables(seq):
    half = TA_DIM // 2
    freqs = ROPE_BASE ** (-np.arange(0, half, 2, dtype=np.float32) / half)
    pos = np.arange(seq)
    lane = np.arange(TA_DIM)
    p = np.where(lane[None, :] < half, (pos // GRID_W)[:, None], (pos % GRID_W)[:, None]).astype(np.float32)
    m = lane % half
    ang = jnp.asarray(p * freqs[m % (half // 2)][None, :])
    sign = jnp.asarray(np.where(m < half // 2, -1.0, 1.0).astype(np.float32))[None, :]
    cos = jnp.concatenate([jnp.cos(ang), jnp.ones((seq, 128 - TA_DIM), F32)], axis=1)
    sin = jnp.concatenate([jnp.sin(ang) * sign, jnp.zeros((seq, 128 - TA_DIM), F32)], axis=1)
    return cos, sin


def _layout_w_in(w):
    d = w.shape[0]
    g = W_GRP
    w = w.astype(BF16)
    na, sg = w[:, 0:3 * g], w[:, 3 * g:5 * g]
    o = 5 * g
    tq, tk, tv = w[:, o:o + TA_DIM], w[:, o + TA_DIM:o + 2 * TA_DIM], w[:, o + 2 * TA_DIM:o + 3 * TA_DIM]
    rest = w[:, o + 3 * TA_DIM:]
    z64 = jnp.zeros((d, 128 - TA_DIM), w.dtype)
    ta = jnp.concatenate([tq, z64, tk, z64, tv, jnp.zeros((d, IN_TILE - 3 * 128 + 128 - TA_DIM), w.dtype)], axis=1)
    return jnp.concatenate([na, sg, rest, ta], axis=1).astype(BF16)


def _layer_params(l, w_in, na_q_gain, na_k_gain, na_rpb, sg_v_gain, sg_ws, sg_bs, ta_q_gain, ta_k_gain,
                  ta_w_out, pl_w, pl_scale, cv_dw_w, cv_dw_b, cv_norm_gain, cv_pw_w, cv_pw_b, w_out,
                  ff_w_up, ff_dw_w, ff_w_down):
    ones64 = jnp.ones((128 - TA_DIM,), F32)
    ws = sg_ws[l].astype(BF16)
    ff = ff_w_down.shape[1]
    p = dict(
        w_in=_layout_w_in(w_in[l]),
        gq=jnp.tile(na_q_gain[l], NA_HEADS)[None, :],
        gk=jnp.tile(na_k_gain[l], NA_HEADS)[None, :],
        gt=jnp.concatenate([ta_q_gain[l], ones64, ta_k_gain[l], ones64])[None, :],
        na_bias=_na_bias_table(na_rpb[l]),
        sg_gain=sg_v_gain[l][None, :],
        sg_wcat=jnp.concatenate([ws[0::2], ws[1::2]], axis=2),
        sg_bsx=jnp.repeat(sg_bs[l].T, W_GRP // SG_HEADS, axis=1),
        ta_w_out=jnp.concatenate([ta_w_out[l], jnp.zeros((128 - TA_DIM, W_GRP), F32)], axis=0).astype(BF16),
        pl_wbd=jax.scipy.linalg.block_diag(*[pl_w[l, g] for g in range(len(PL_WINDOWS))]).astype(BF16),
        pl_scale=pl_scale[l][None, :],
        cv_dw_w=cv_dw_w[l], cv_dw_b=cv_dw_b[l][None, :], cv_ng=cv_norm_gain[l][None, :],
        cv_pw_w=cv_pw_w[l].astype(BF16), cv_pw_b=cv_pw_b[l][None, :],
        w_out=w_out[l].astype(BF16),
        ff_w_up=ff_w_up[l].astype(BF16), ff_dw=ff_dw_w[l], ff_wd=ff_w_down[l].astype(BF16),
    )
    return p


def _smat():
    idx = np.arange(IN_TILE) // HEAD_DIM
    return jnp.asarray((idx[:, None] == idx[None, :]).astype(np.float32)).astype(BF16)


def _mixers(p, z, na_o, ta_o, *, tpb):
    sg_o = _sg(z, ta_o, p["sg_gain"], p["sg_wcat"], p["sg_bsx"], p["ta_w_out"], tpb=tpb)
    pl_o = _pool(z, p["pl_wbd"], p["pl_scale"], tpb=tpb)
    cv_o = _conv(z, p["cv_dw_w"], p["cv_dw_b"], p["cv_ng"], p["cv_pw_w"], p["cv_pw_b"], tpb=tpb)
    return [na_o, sg_o, pl_o, cv_o]


def _tail(p, ys, x2d, mods, *, tpb):
    x1, hn = _outproj(ys, p["w_out"], x2d, mods[2], mods[3], mods[4], tpb=tpb)
    return _ffn(hn, x1, mods[5], p["ff_w_up"], p["ff_wd"], p["ff_dw"], tpb=tpb)


def kernel(x, c, ctx, c_ctx, w_mod, b_mod, w_in, na_q_gain, na_k_gain, na_rpb, sg_v_gain, sg_ws, sg_bs,
           ta_q_gain, ta_k_gain, ta_w_out, pl_w, pl_scale, cv_dw_w, cv_dw_b, cv_norm_gain, cv_pw_w, cv_pw_b,
           w_out, ff_w_up, ff_dw_w, ff_w_down):
    nb, seq, d = x.shape
    lc = ctx.shape[1]
    depth = w_mod.shape[0]
    mods_all = _mod_vectors(c, c_ctx, w_mod, b_mod).reshape(depth, 8, 6, d)
    smat = _smat()
    cos_x, sin_x = _rope_tables(seq)
    cos_c, sin_c = jnp.ones((lc, 128), F32), jnp.zeros((lc, 128), F32)

    x2d = x.reshape(nb * seq, d)
    xc2d = ctx.reshape(nb * lc, d)
    for l in range(depth):
        last = l == depth - 1
        p = _layer_params(l, w_in, na_q_gain, na_k_gain, na_rpb, sg_v_gain, sg_ws, sg_bs, ta_q_gain, ta_k_gain,
                          ta_w_out, pl_w, pl_scale, cv_dw_w, cv_dw_b, cv_norm_gain, cv_pw_w, cv_pw_b, w_out,
                          ff_w_up, ff_dw_w, ff_w_down)
        mx = [mods_all[l, :nb, k][:, None, :] for k in range(6)]
        mc = [jnp.broadcast_to(mods_all[l, nb, k][None, None, :], (nb, 1, d)) for k in range(6)]
        zc = _inproj(xc2d, mc[0], mc[1], p["w_in"], smat, p["gq"], p["gk"], p["gt"], cos_c, sin_c, tpb=lc)
        z = _inproj(x2d, mx[0], mx[1], p["w_in"], smat, p["gq"], p["gk"], p["gt"], cos_x, sin_x, tpb=seq)
        na_o = _na_x(z, zc, p["na_bias"], nb=nb, seq=seq, lc=lc)
        ta_o = _ta(z, zc, nb=nb, lq=seq, le=lc)
        ys = _mixers(p, z, na_o, ta_o, tpb=seq)
        if not last:
            na_oc = _na_ctx(zc, nb=nb, lc=lc)
            ta_oc = _ta(zc, None, nb=nb, lq=lc, le=0)
            ysc = _mixers(p, zc, na_oc, ta_oc, tpb=lc)
            xc2d = _tail(p, ysc, xc2d, mc, tpb=lc)
        x2d = _tail(p, ys, x2d, mx, tpb=seq)
    return x2d.reshape(nb, seq, d)
```

```python
import functools

import numpy as np
import jax
import jax.numpy as jnp
from jax import lax
from jax.experimental import pallas as pl
from jax.experimental.pallas import tpu as pltpu

F32 = jnp.float32
BF16 = jnp.bfloat16

D_MODEL = 2048
GRID_W = 64
HEAD_DIM = 64
W_GRP = D_MODEL // 4
NA_HEADS = W_GRP // HEAD_DIM
NA_WIN_R = 8
NA_WIN_C = 16
NA_QROWS = 4
NA_BAND = NA_WIN_R + NA_QROWS
NA_SUB = 8
LOG2E = 1.4426950408889634
SG_HEADS = 8
SG_CHUNK = 128
TA_DIM = 64
PL_WINDOWS = (2, 4, 8, 16)
CV_KERNEL = 31
FF_KERNEL = 3
ROPE_BASE = 10000.0
EPS = 1e-6
HALO = 16
SUBLANES = 8
IN_TILE = 512
N_IN_TILES = 9
Z_COLS = 8 * IN_TILE
ZB_NA_Q, ZB_NA_K, ZB_NA_V = 0, 4, 8
ZB_TA_Q, ZB_TA_K, ZB_TA_V = 28, 29, 30
ZT_SG_U, ZT_SG_V, ZT_POOL, ZT_CONV, ZT_TA = 3, 4, 5, 6, 7
NEG_BIG = -1e30
NT_DIMS = (((1,), (1,)), ((), ()))
VMEM_LIMIT_MB = 56


def _cparams(sem, vmem_mb=VMEM_LIMIT_MB):
    return pltpu.CompilerParams(dimension_semantics=sem, vmem_limit_bytes=vmem_mb * 1024 * 1024)


def _dot(a, b):
    return jnp.dot(a, b, preferred_element_type=F32)


def _dot_nt(a, b):
    return lax.dot_general(a, b, NT_DIMS, preferred_element_type=F32)


def _mod_kernel(c_ref, w_ref, b_ref, o_ref):
    cv = c_ref[...]
    s = cv * jax.nn.sigmoid(cv)
    o_ref[0] = _dot(s.astype(BF16), w_ref[0].astype(BF16)) + b_ref[0]


def _mod_vectors(c, c_ctx, w_mod, b_mod):
    depth, d, n = w_mod.shape
    nb = c.shape[0]
    assert nb + 1 <= 8
    cv = jnp.zeros((8, d), F32).at[:nb].set(c).at[nb].set(c_ctx)
    tn = 1024
    return pl.pallas_call(
        _mod_kernel,
        grid=(depth, n // tn),
        in_specs=[
            pl.BlockSpec((8, d), lambda l, k: (0, 0)),
            pl.BlockSpec((1, d, tn), lambda l, k: (l, 0, k)),
            pl.BlockSpec((1, 1, tn), lambda l, k: (l, 0, k)),
        ],
        out_specs=pl.BlockSpec((1, 8, tn), lambda l, k: (l, 0, k)),
        out_shape=jax.ShapeDtypeStruct((depth, 8, n), F32),
        compiler_params=_cparams(("parallel", "parallel"), 40),
        name="mod_vectors",
    )(cv, w_mod, b_mod.reshape(depth, 1, n))


def _rope(t, cos, sin):
    lane = lax.broadcasted_iota(jnp.int32, t.shape, 1)
    partner = jnp.where((lane % 32) < 16, pltpu.roll(t, 128 - 16, 1), pltpu.roll(t, 16, 1))
    return t * cos + partner * sin


def _inproj_kernel(x_ref, sh_ref, sc_ref, w_ref, smat_ref, gq_ref, gk_ref, gt_ref, cos_ref, sin_ref,
                   z_ref, xn_scr):
    half = x_ref.shape[0] // 2
    for r in range(2):
        _inproj_rows(slice(r * half, (r + 1) * half), x_ref, sh_ref, sc_ref, w_ref, smat_ref, gq_ref, gk_ref,
                     gt_ref, cos_ref, sin_ref, z_ref, xn_scr)


def _inproj_rows(rs, x_ref, sh_ref, sc_ref, w_ref, smat_ref, gq_ref, gk_ref, gt_ref, cos_ref, sin_ref,
                 z_ref, xn_scr):
    x = x_ref[rs, :]
    ms = jnp.mean(x * x, axis=-1, keepdims=True)
    xn_scr[rs, :] = (x * lax.rsqrt(ms + EPS) * (1.0 + sc_ref[0]) + sh_ref[0]).astype(BF16)

    def proj(j):
        return _dot(xn_scr[rs, :], w_ref[:, j * IN_TILE:(j + 1) * IN_TILE])

    def put(t, val):
        z_ref[rs, t * IN_TILE:(t + 1) * IN_TILE] = val.astype(BF16)

    def head_rms(a, gain):
        sq = a * a
        hi = sq.astype(BF16)
        lo = (sq - hi.astype(F32)).astype(BF16)
        ss = _dot(hi, smat_ref[...]) + _dot(lo, smat_ref[...])
        return a * lax.rsqrt(ss * (1.0 / HEAD_DIM) + EPS) * gain

    acc = proj(8)

    def nrm(t, g):
        ms_t = jnp.sum(t * t, axis=-1, keepdims=True) * (1.0 / TA_DIM)
        return t * lax.rsqrt(ms_t + EPS) * g
    cos = cos_ref[rs, :]
    sin = sin_ref[rs, :]
    tq = _rope(nrm(acc[:, 0:128], gt_ref[:, 0:128]), cos, sin) * (TA_DIM ** -0.5 * LOG2E)
    tk = _rope(nrm(acc[:, 128:256], gt_ref[:, 128:256]), cos, sin)
    tv = acc[:, 256:384]
    lane = lax.broadcasted_iota(jnp.int32, tv.shape, 1)
    c0 = ZT_TA * IN_TILE
    z_ref[rs, c0:c0 + 128] = tq.astype(BF16)
    z_ref[rs, c0 + 128:c0 + 256] = tk.astype(BF16)
    z_ref[rs, c0 + 256:c0 + 384] = jnp.where(lane == TA_DIM, 1.0, tv).astype(BF16)
    z_ref[rs, c0 + 384:c0 + 512] = acc[:, 384:512].astype(BF16)

    put(0, head_rms(proj(0), gq_ref[...]) * (HEAD_DIM ** -0.5 * LOG2E))
    put(1, head_rms(proj(1), gk_ref[...]))
    put(ZT_SG_U, jax.nn.gelu(proj(3)))
    put(ZT_SG_V, jax.nn.gelu(proj(4)))
    put(ZT_CONV, proj(6) * jax.nn.sigmoid(proj(7)))
    put(2, proj(2))
    put(ZT_POOL, proj(5))


def _inproj(x2d, sh, sc, w_r, layer, smat, gq, gk, gt, cos_t, sin_t, *, tpb):
    t, d = x2d.shape
    tm = min(512, tpb)
    tiles_pb = tpb // tm
    const = lambda i: (0, 0)
    return pl.pallas_call(
        _inproj_kernel,
        grid=(t // tm,),
        in_specs=[
            pl.BlockSpec((tm, d), lambda i: (i, 0)),
            pl.BlockSpec((1, 1, d), lambda i: (i // tiles_pb, 0, 0)),
            pl.BlockSpec((1, 1, d), lambda i: (i // tiles_pb, 0, 0)),
            pl.BlockSpec((None, d, N_IN_TILES * IN_TILE), lambda i: (layer, 0, 0),
                         pipeline_mode=pl.Buffered(1)),
            pl.BlockSpec((IN_TILE, IN_TILE), const),
            pl.BlockSpec((1, IN_TILE), const),
            pl.BlockSpec((1, IN_TILE), const),
            pl.BlockSpec((1, 256), const),
            pl.BlockSpec((tm, 128), lambda i: (i % tiles_pb, 0)),
            pl.BlockSpec((tm, 128), lambda i: (i % tiles_pb, 0)),
        ],
        out_specs=pl.BlockSpec((tm, Z_COLS), lambda i: (i, 0)),
        out_shape=jax.ShapeDtypeStruct((t, Z_COLS), BF16),
        scratch_shapes=[pltpu.VMEM((tm, d), BF16)],
        compiler_params=_cparams(("parallel",)),
        name="in_proj",
    )(x2d, sh, sc, w_r, smat, gq, gk, gt, cos_t, sin_t)


def _stack_heads(q):
    lane = lax.broadcasted_iota(jnp.int32, q.shape, 1)
    zero = jnp.zeros_like(q)
    return jnp.concatenate([jnp.where(lane < HEAD_DIM, q, zero), jnp.where(lane >= HEAD_DIM, q, zero)], axis=0)


def _unstack_heads(o2):
    m = o2.shape[0] // 2
    lane = lax.broadcasted_iota(jnp.int32, (m, 128), 1)
    return jnp.where(lane < HEAD_DIM, o2[:m], o2[m:])


def _na_x_kernel(q_ref, k_ref, v_ref, kc_ref, vc_ref, b_ref, o_ref, *, rows):
    tq = NA_QROWS * GRID_W
    nkeys = NA_BAND * GRID_W
    nrb = rows // NA_QROWS
    kc = kc_ref[...]
    vc = vc_ref[...]
    for sub in range(NA_SUB):
        rb = pl.program_id(2) * NA_SUB + sub
        rs = jnp.clip(rb * NA_QROWS - NA_WIN_R // 2, 0, rows - NA_BAND)
        start = pl.multiple_of(rs * GRID_W, GRID_W)
        case = jnp.where(rb == 0, 0, jnp.where(rb == nrb - 1, 2, 1))
        kw = k_ref[pl.ds(start, nkeys), :]
        vw = v_ref[pl.ds(start, nkeys), :]
        q2 = _stack_heads(q_ref[sub * tq:(sub + 1) * tq, :])
        s1 = _dot_nt(q2, kw) + b_ref[case, 0]
        s2 = _dot_nt(q2, kc)
        m = jnp.maximum(jnp.max(s1, axis=-1, keepdims=True), jnp.max(s2, axis=-1, keepdims=True))
        p1 = jnp.exp2(s1 - m)
        p2 = jnp.exp2(s2 - m)
        l = jnp.sum(p1, axis=-1, keepdims=True) + jnp.sum(p2, axis=-1, keepdims=True)
        o2 = _dot(p1.astype(BF16), vw) + _dot(p2.astype(BF16), vc)
        o_ref[sub * tq:(sub + 1) * tq, :] = _unstack_heads(o2 / l).astype(BF16)


def _na_x(z, zc, bias, *, nb, seq, lc):
    rows = seq // GRID_W
    assert rows % (NA_QROWS * NA_SUB) == 0 and rows >= NA_BAND
    nsteps = rows // (NA_QROWS * NA_SUB)
    tq = NA_QROWS * GRID_W
    return pl.pallas_call(
        functools.partial(_na_x_kernel, rows=rows),
        grid=(nb, NA_HEADS // 2, nsteps),
        in_specs=[
            pl.BlockSpec((NA_SUB * tq, 128), lambda b, hp, st: (b * nsteps + st, ZB_NA_Q + hp)),
            pl.BlockSpec((seq, 128), lambda b, hp, st: (b, ZB_NA_K + hp)),
            pl.BlockSpec((seq, 128), lambda b, hp, st: (b, ZB_NA_V + hp)),
            pl.BlockSpec((lc, 128), lambda b, hp, st: (b, ZB_NA_K + hp)),
            pl.BlockSpec((lc, 128), lambda b, hp, st: (b, ZB_NA_V + hp)),
            pl.BlockSpec((3, 1, 2 * tq, NA_BAND * GRID_W), lambda b, hp, st: (0, hp, 0, 0)),
        ],
        out_specs=pl.BlockSpec((NA_SUB * tq, 128), lambda b, hp, st: (b * nsteps + st, hp)),
        out_shape=jax.ShapeDtypeStruct((nb * seq, W_GRP), BF16),
        compiler_params=_cparams(("parallel", "parallel", "arbitrary")),
        name="na_attn",
    )(z, z, z, zc, zc, bias)


def _na_bias_table(rpb):
    h = rpb.shape[0]
    nq = NA_QROWS * GRID_W
    nk = NA_BAND * GRID_W
    qc, kc = np.arange(GRID_W), np.arange(GRID_W)
    c0 = np.clip(qc - NA_WIN_C // 2, 0, GRID_W - NA_WIN_C)
    col_ok = (kc[None, :] >= c0[:, None]) & (kc[None, :] < c0[:, None] + NA_WIN_C)
    cidx = np.clip(kc[None, :] - qc[:, None] + NA_WIN_C - 1, 0, 2 * NA_WIN_C - 2)
    col_sel = jnp.asarray((cidx.reshape(-1)[None, :] == np.arange(2 * NA_WIN_C - 1)[:, None]).astype(np.float32))
    qr, wr = np.arange(NA_QROWS), np.arange(NA_BAND)
    half = NA_WIN_R // 2
    cases = ((0, np.zeros(NA_QROWS, int)), (half, np.arange(NA_QROWS)), (NA_WIN_R, np.full(NA_QROWS, half)))
    ridx, valid = [], []
    for dq, r0 in cases:
        row_ok = (wr[None, :] >= r0[:, None]) & (wr[None, :] < r0[:, None] + NA_WIN_R)
        ridx.append(np.clip(wr[None, :] - (dq + qr[:, None]) + NA_WIN_R - 1, 0, 2 * NA_WIN_R - 2).reshape(-1))
        valid.append(row_ok[:, None, :, None] & col_ok[None, :, None, :])
    rows = rpb[:, np.stack(ridx), :] * LOG2E
    blocks = jnp.einsum("hbrc,cx->bhrx", rows, col_sel, precision=lax.Precision.HIGHEST)
    blocks = blocks.reshape(3, h, NA_QROWS, NA_BAND, GRID_W, GRID_W).transpose(0, 1, 2, 4, 3, 5)
    tab = jnp.where(jnp.asarray(np.stack(valid))[:, None], blocks, NEG_BIG)
    return tab.reshape(3, h // 2, 2 * nq, nk)


def _na_ctx_kernel(q_ref, k_ref, v_ref, o_ref):
    q2 = _stack_heads(q_ref[...])
    s = _dot_nt(q2, k_ref[...])
    m = jnp.max(s, axis=-1, keepdims=True)
    p = jnp.exp2(s - m)
    l = jnp.sum(p, axis=-1, keepdims=True)
    o2 = _dot(p.astype(BF16), v_ref[...])
    o_ref[...] = _unstack_heads(o2 / l).astype(BF16)


def _na_ctx(zc, *, nb, lc):
    return pl.pallas_call(
        _na_ctx_kernel,
        grid=(nb, NA_HEADS // 2),
        in_specs=[
            pl.BlockSpec((lc, 128), lambda b, hp: (b, ZB_NA_Q + hp)),
            pl.BlockSpec((lc, 128), lambda b, hp: (b, ZB_NA_K + hp)),
            pl.BlockSpec((lc, 128), lambda b, hp: (b, ZB_NA_V + hp)),
        ],
        out_specs=pl.BlockSpec((lc, 128), lambda b, hp: (b, hp)),
        out_shape=jax.ShapeDtypeStruct((nb * lc, W_GRP), BF16),
        compiler_params=_cparams(("parallel", "parallel")),
        name="na_attn_ctx",
    )(zc, zc, zc)


def _ta_kernel(*refs, tk, nk, has_extra):
    if has_extra:
        q_ref, k_ref, v_ref, ke_ref, ve_ref, o_ref = refs
    else:
        q_ref, k_ref, v_ref, o_ref = refs
    q = q_ref[...]
    tq = q.shape[0]

    def step(kb, vb, carry):
        m, acc = carry
        s = _dot_nt(q, kb)
        mn = jnp.maximum(m, jnp.max(s, axis=-1, keepdims=True))
        p = jnp.exp2(s - mn).astype(BF16)
        acc = jnp.exp2(m - mn) * acc + _dot(p, vb)
        return mn, acc

    def body(c, carry):
        st = pl.multiple_of(c * tk, tk)
        return step(k_ref[pl.ds(st, tk), :], v_ref[pl.ds(st, tk), :], carry)

    carry = (jnp.full((tq, 1), NEG_BIG, F32), jnp.zeros((tq, 128), F32))
    carry = lax.fori_loop(0, nk, body, carry, unroll=8 if nk % 8 == 0 else 1)
    if has_extra:
        carry = step(ke_ref[...], ve_ref[...], carry)
    _, acc = carry
    lane = lax.broadcasted_iota(jnp.int32, acc.shape, 1)
    l = jnp.sum(jnp.where(lane == TA_DIM, acc, 0.0), axis=-1, keepdims=True)
    o_ref[...] = jnp.where(lane < TA_DIM, acc / l, 0.0).astype(BF16)


def _ta(zq, zextra, *, nb, lq, le):
    tq = min(1024, lq)
    tk = min(1024, lq)
    nqb = lq // tq
    has_extra = zextra is not None
    in_specs = [
        pl.BlockSpec((tq, 128), lambda b, qi: (b * nqb + qi, ZB_TA_Q)),
        pl.BlockSpec((lq, 128), lambda b, qi: (b, ZB_TA_K)),
        pl.BlockSpec((lq, 128), lambda b, qi: (b, ZB_TA_V)),
    ]
    args = [zq, zq, zq]
    if has_extra:
        in_specs += [
            pl.BlockSpec((le, 128), lambda b, qi: (b, ZB_TA_K)),
            pl.BlockSpec((le, 128), lambda b, qi: (b, ZB_TA_V)),
        ]
        args += [zextra, zextra]
    return pl.pallas_call(
        functools.partial(_ta_kernel, tk=tk, nk=lq // tk, has_extra=has_extra),
        grid=(nb, nqb),
        in_specs=in_specs,
        out_specs=pl.BlockSpec((tq, 128), lambda b, qi: (b * nqb + qi, 0)),
        out_shape=jax.ShapeDtypeStruct((nb * lq, 128), BF16),
        compiler_params=_cparams(("parallel", "arbitrary")),
        name="ta_attn",
    )(*args)


def _sg_kernel(u_ref, gv_ref, tao_ref, gain_ref, wcat_ref, bsx_ref, two_ref, o_ref, v_scr):
    tm = u_ref.shape[0]
    gv = gv_ref[...].astype(F32)
    ms = jnp.mean(gv * gv, axis=-1, keepdims=True)
    v_scr[...] = (gv * lax.rsqrt(ms + EPS) * gain_ref[...]).astype(BF16)
    lane = lax.broadcasted_iota(jnp.int32, (SG_CHUNK, 128), 1)
    zero = jnp.zeros((SG_CHUNK, 128), BF16)

    def chunk(c, carry):
        r0 = pl.multiple_of(c * SG_CHUNK, SG_CHUNK)
        rows = pl.ds(r0, SG_CHUNK)
        outs = []
        for p in range(SG_HEADS // 2):
            vs = v_scr[rows, p * 128:(p + 1) * 128]
            rhs = jnp.concatenate([jnp.where(lane < 64, vs, zero), jnp.where(lane >= 64, vs, zero)], axis=0)
            outs.append(_dot(wcat_ref[p], rhs))
        s = jnp.concatenate(outs, axis=1) + bsx_ref[...] + _dot(tao_ref[rows, :], two_ref[...])
        o_ref[rows, :] = (u_ref[rows, :].astype(F32) * s).astype(BF16)
        return carry

    lax.fori_loop(0, tm // SG_CHUNK, chunk, 0)


def _sg(z, tao, gain, wcat, bsx, two, *, tpb):
    t = z.shape[0]
    tm = min(1024, tpb)
    const2 = lambda i: (0, 0)
    return pl.pallas_call(
        _sg_kernel,
        grid=(t // tm,),
        in_specs=[
            pl.BlockSpec((tm, W_GRP), lambda i: (i, ZT_SG_U)),
            pl.BlockSpec((tm, W_GRP), lambda i: (i, ZT_SG_V)),
            pl.BlockSpec((tm, 128), lambda i: (i, 0)),
            pl.BlockSpec((1, W_GRP), const2),
            pl.BlockSpec((SG_HEADS // 2, SG_CHUNK, 2 * SG_CHUNK), lambda i: (0, 0, 0)),
            pl.BlockSpec((SG_CHUNK, W_GRP), const2),
            pl.BlockSpec((128, W_GRP), const2),
        ],
        out_specs=pl.BlockSpec((tm, W_GRP), lambda i: (i, 0)),
        out_shape=jax.ShapeDtypeStruct((t, W_GRP), BF16),
        scratch_shapes=[pltpu.VMEM((tm, W_GRP), BF16)],
        compiler_params=_cparams(("parallel",)),
        name="sg_unit",
    )(z, z, tao, gain, wcat, bsx, two)


def _halo_specs(tm, t, width, col):
    hb = tm // HALO
    last = t // HALO - 1
    return [
        pl.BlockSpec((tm, width), lambda i: (i, col)),
        pl.BlockSpec((HALO, width), lambda i: (jnp.maximum(i * hb - 1, 0), col)),
        pl.BlockSpec((HALO, width), lambda i: (jnp.minimum((i + 1) * hb, last), col)),
    ]


def _fill_ext(ext_scr, main_ref, prev_ref, next_ref, it, tiles_pb):
    tm = main_ref.shape[0]
    zero = jnp.zeros(prev_ref.shape, F32)
    ext_scr[0:HALO, :] = jnp.where(it == 0, zero, prev_ref[...].astype(F32))
    ext_scr[HALO:HALO + tm, :] = main_ref[...].astype(F32)
    ext_scr[HALO + tm:2 * HALO + tm, :] = jnp.where(it == tiles_pb - 1, zero, next_ref[...].astype(F32))


def _pool_kernel(z_ref, zp_ref, zn_ref, wbd_ref, scale_ref, o_ref, ext_scr, d_scr, *, tiles_pb, seq):
    tm = z_ref.shape[0]
    it = pl.program_id(0) % tiles_pb
    _fill_ext(ext_scr, z_ref, zp_ref, zn_ref, it, tiles_pb)
    rc = 128

    for c in range(tm // rc):
        r0 = c * rc
        pos = it * tm + r0 + lax.broadcasted_iota(jnp.int32, (rc, 1), 0)
        for g, w in enumerate(PL_WINDOWS):
            cols = slice(g * 128, (g + 1) * 128)
            lo = r0 + HALO - w // 2
            acc = ext_scr[lo:lo + rc, cols]
            for s in range(1, w):
                acc = acc + ext_scr[lo + s:lo + s + rc, cols]
            cnt = (jnp.minimum(pos + w // 2, seq) - jnp.maximum(pos - w // 2, 0)).astype(F32)
            d = acc / cnt - ext_scr[r0 + HALO:r0 + HALO + rc, cols]
            d_scr[r0:r0 + rc, cols] = d.astype(BF16)
    o_ref[...] = (_dot(d_scr[...], wbd_ref[...]) * scale_ref[...]).astype(BF16)


def _pool(z, wbd, scale, *, tpb):
    t = z.shape[0]
    tm = min(1024, tpb)
    const2 = lambda i: (0, 0)
    return pl.pallas_call(
        functools.partial(_pool_kernel, tiles_pb=tpb // tm, seq=tpb),
        grid=(t // tm,),
        in_specs=_halo_specs(tm, t, W_GRP, ZT_POOL) + [
            pl.BlockSpec((W_GRP, W_GRP), const2),
            pl.BlockSpec((1, W_GRP), const2),
        ],
        out_specs=pl.BlockSpec((tm, W_GRP), lambda i: (i, 0)),
        out_shape=jax.ShapeDtypeStruct((t, W_GRP), BF16),
        scratch_shapes=[pltpu.VMEM((tm + 2 * HALO, W_GRP), F32), pltpu.VMEM((tm, W_GRP), BF16)],
        compiler_params=_cparams(("parallel",)),
        name="pool_mixer",
    )(z, z, z, wbd, scale)


def _conv_kernel(h_ref, hp_ref, hn_ref, dww_ref, dwb_ref, ng_ref, pww_ref, pwb_ref, o_ref, ext_scr, sh_scr,
                 a_scr, *, tiles_pb):
    tm = h_ref.shape[0]
    it = pl.program_id(0) % tiles_pb
    _fill_ext(ext_scr, h_ref, hp_ref, hn_ref, it, tiles_pb)
    rc = 64
    base = HALO - (CV_KERNEL - 1) // 2
    nsh = sh_scr.shape[1]
    for s in range(1, SUBLANES):
        sh_scr[s - 1] = ext_scr[s:s + nsh, :]

    for c in range(tm // rc):
        r0 = c * rc
        acc = jnp.broadcast_to(dwb_ref[...], (rc, W_GRP))
        for k in range(CV_KERNEL):
            a, s = divmod(base + k, SUBLANES)
            lo = r0 + a * SUBLANES
            rows = ext_scr[lo:lo + rc, :] if s == 0 else sh_scr[s - 1, lo:lo + rc, :]
            acc = acc + dww_ref[k:k + 1, :] * rows
        ms = jnp.mean(acc * acc, axis=-1, keepdims=True)
        hh = acc * lax.rsqrt(ms + EPS) * ng_ref[...]
        a_scr[r0:r0 + rc, :] = (hh * jax.nn.sigmoid(hh)).astype(BF16)
    o_ref[...] = (_dot(a_scr[...], pww_ref[...]) + pwb_ref[...]).astype(BF16)


def _conv(z, dww, dwb, ng, pww, pwb, *, tpb):
    t = z.shape[0]
    tm = min(512, tpb)
    const2 = lambda i: (0, 0)
    return pl.pallas_call(
        functools.partial(_conv_kernel, tiles_pb=tpb // tm),
        grid=(t // tm,),
        in_specs=_halo_specs(tm, t, W_GRP, ZT_CONV) + [
            pl.BlockSpec((CV_KERNEL, W_GRP), const2),
            pl.BlockSpec((1, W_GRP), const2),
            pl.BlockSpec((1, W_GRP), const2),
            pl.BlockSpec((W_GRP, W_GRP), const2),
            pl.BlockSpec((1, W_GRP), const2),
        ],
        out_specs=pl.BlockSpec((tm, W_GRP), lambda i: (i, 0)),
        out_shape=jax.ShapeDtypeStruct((t, W_GRP), BF16),
        scratch_shapes=[pltpu.VMEM((tm + 2 * HALO, W_GRP), F32),
                        pltpu.VMEM((SUBLANES - 1, tm + 2 * HALO - SUBLANES, W_GRP), F32),
                        pltpu.VMEM((tm, W_GRP), BF16)],
        compiler_params=_cparams(("parallel",)),
        name="conformer_conv",
    )(z, z, z, dww, dwb, ng, pww, pwb)


def _outproj_kernel(yna_ref, ysg_ref, ypl_ref, ycv_ref, w_ref, x_ref, g1_ref, sh2_ref, sc2_ref, x1_ref, hn_ref):
    tm, d = x_ref.shape
    y = jnp.concatenate([yna_ref[...], ysg_ref[...], ypl_ref[...], ycv_ref[...]], axis=1)
    nc = 512
    ssq = jnp.zeros((tm, 1), F32)
    for n in range(d // nc):
        cs = slice(n * nc, (n + 1) * nc)
        x1 = x_ref[:, cs] + g1_ref[0][:, cs] * _dot(y, w_ref[:, cs])
        x1_ref[:, cs] = x1
        ssq = ssq + jnp.sum(x1 * x1, axis=-1, keepdims=True)
    r = lax.rsqrt(ssq * (1.0 / d) + EPS)
    for n in range(d // nc):
        cs = slice(n * nc, (n + 1) * nc)
        hn_ref[:, cs] = (x1_ref[:, cs] * r * (1.0 + sc2_ref[0][:, cs]) + sh2_ref[0][:, cs]).astype(BF16)


def _outproj(ys, w_out, layer, x2d, g1, sh2, sc2, *, tpb):
    t, d = x2d.shape
    tm = min(512, tpb)
    tiles_pb = tpb // tm
    row = lambda i: (i // tiles_pb, 0, 0)
    return pl.pallas_call(
        _outproj_kernel,
        grid=(t // tm,),
        in_specs=[pl.BlockSpec((tm, W_GRP), lambda i: (i, 0))] * 4 + [
            pl.BlockSpec((None, d, d), lambda i: (layer, 0, 0)),
            pl.BlockSpec((tm, d), lambda i: (i, 0)),
            pl.BlockSpec((1, 1, d), row),
            pl.BlockSpec((1, 1, d), row),
            pl.BlockSpec((1, 1, d), row),
        ],
        out_specs=[pl.BlockSpec((tm, d), lambda i: (i, 0)), pl.BlockSpec((tm, d), lambda i: (i, 0))],
        out_shape=[jax.ShapeDtypeStruct((t, d), F32), jax.ShapeDtypeStruct((t, d), BF16)],
        compiler_params=_cparams(("parallel",)),
        name="out_proj",
    )(*ys, w_out, x2d, g1, sh2, sc2)


def _ffn_kernel(hn_ref, hp_ref, hx_ref, x1_ref, g2_ref, wg_ref, wv_ref, wd_ref, dw_ref, o_ref, ext_scr, g_scr,
                *, tiles_pb, nx):
    tm, d = hn_ref.shape
    j = pl.program_id(1)
    it = pl.program_id(0) % tiles_pb

    @pl.when(j == 0)
    def _():
        zero = jnp.zeros((HALO, d), BF16)
        ext_scr[0:HALO, :] = jnp.where(it == 0, zero, hp_ref[...])
        ext_scr[HALO:HALO + tm, :] = hn_ref[...]
        ext_scr[HALO + tm:2 * HALO + tm, :] = jnp.where(it == tiles_pb - 1, zero, hx_ref[...])
        o_ref[...] = jnp.zeros(o_ref.shape, F32)

    g_scr[...] = _dot(ext_scr[...], wg_ref[...])
    val = _dot(hn_ref[...], wv_ref[...])
    c = (dw_ref[0:1, :] * g_scr[HALO - 1:HALO - 1 + tm, :] + dw_ref[1:2, :] * g_scr[HALO:HALO + tm, :]
         + dw_ref[2:3, :] * g_scr[HALO + 1:HALO + 1 + tm, :])
    a = (c * jax.nn.sigmoid(c) * val).astype(BF16)
    nc = 512
    for n in range(d // nc):
        cs = slice(n * nc, (n + 1) * nc)
        o_ref[:, cs] += g2_ref[0][:, cs] * _dot(a, wd_ref[:, cs])

    xc = d // nx
    for s in range(nx):
        @pl.when(j == s)
        def _(s=s):
            o_ref[:, s * xc:(s + 1) * xc] += x1_ref[...]


def _ffn(hn, x1, g2, w_up, wd, layer, dw, *, tpb):
    t, d = x1.shape
    ff = wd.shape[1]
    tm = min(1024, tpb)
    tiles_pb = tpb // tm
    tf = 512
    nj = ff // tf
    nx = 8
    assert nj >= nx
    hb = tm // HALO
    last = t // HALO - 1
    return pl.pallas_call(
        functools.partial(_ffn_kernel, tiles_pb=tiles_pb, nx=nx),
        grid=(t // tm, nj),
        in_specs=[
            pl.BlockSpec((tm, d), lambda i, j: (i, 0)),
            pl.BlockSpec((HALO, d), lambda i, j: (jnp.maximum(i * hb - 1, 0), 0)),
            pl.BlockSpec((HALO, d), lambda i, j: (jnp.minimum((i + 1) * hb, last), 0)),
            pl.BlockSpec((tm, d // nx), lambda i, j: (i, jnp.minimum(j, nx - 1))),
            pl.BlockSpec((1, 1, d), lambda i, j: (i // tiles_pb, 0, 0)),
            pl.BlockSpec((None, d, tf), lambda i, j: (layer, 0, j)),
            pl.BlockSpec((None, d, tf), lambda i, j: (layer, 0, nj + j)),
            pl.BlockSpec((None, tf, d), lambda i, j: (layer, j, 0)),
            pl.BlockSpec((FF_KERNEL, tf), lambda i, j: (0, j)),
        ],
        out_specs=pl.BlockSpec((tm, d), lambda i, j: (i, 0)),
        out_shape=jax.ShapeDtypeStruct((t, d), F32),
        scratch_shapes=[pltpu.VMEM((tm + 2 * HALO, d), BF16), pltpu.VMEM((tm + 2 * HALO, tf), F32)],
        compiler_params=_cparams(("parallel", "arbitrary")),
        name="conv_glu_ffn",
    )(hn, hn, hn, x1, g2, w_up, w_up, wd, dw)


def _rope_tables(seq):
    half = TA_DIM // 2
    freqs = ROPE_BASE ** (-np.arange(0, half, 2, dtype=np.float32) / half)
    pos = np.arange(seq)
    lane = np.arange(TA_DIM)
    p = np.where(lane[None, :] < half, (pos // GRID_W)[:, None], (pos % GRID_W)[:, None]).astype(np.float32)
    m = lane % half
    ang = jnp.asarray(p * freqs[m % (half // 2)][None, :])
    sign = jnp.asarray(np.where(m < half // 2, -1.0, 1.0).astype(np.float32))[None, :]
    cos = jnp.concatenate([jnp.cos(ang), jnp.ones((seq, 128 - TA_DIM), F32)], axis=1)
    sin = jnp.concatenate([jnp.sin(ang) * sign, jnp.zeros((seq, 128 - TA_DIM), F32)], axis=1)
    return cos, sin


def _layout_w_in(w):
    lead = w.shape[:-1]
    g = W_GRP
    w = w.astype(BF16)
    o = 5 * g
    tq, tk, tv = (w[..., o + i * TA_DIM:o + (i + 1) * TA_DIM] for i in range(3))
    z64 = jnp.zeros(lead + (128 - TA_DIM,), BF16)
    tail = jnp.zeros(lead + (IN_TILE - 3 * 128 + 128 - TA_DIM,), BF16)
    return jnp.concatenate([w[..., :o], w[..., o + 3 * TA_DIM:], tq, z64, tk, z64, tv, tail], axis=-1)


def _layer_params(l, w_in, na_q_gain, na_k_gain, na_rpb, sg_v_gain, sg_ws, sg_bs, ta_q_gain, ta_k_gain,
                  ta_w_out, pl_w, pl_scale, cv_dw_w, cv_dw_b, cv_norm_gain, cv_pw_w, cv_pw_b, w_out,
                  ff_w_up, ff_dw_w, ff_w_down):
    ones64 = jnp.ones((128 - TA_DIM,), F32)
    ws = sg_ws[l].astype(BF16)
    ff = ff_w_down.shape[1]
    p = dict(
        gq=jnp.tile(na_q_gain[l], NA_HEADS)[None, :],
        gk=jnp.tile(na_k_gain[l], NA_HEADS)[None, :],
        gt=jnp.concatenate([ta_q_gain[l], ones64, ta_k_gain[l], ones64])[None, :],
        na_bias=_na_bias_table(na_rpb[l]),
        sg_gain=sg_v_gain[l][None, :],
        sg_wcat=jnp.concatenate([ws[0::2], ws[1::2]], axis=2),
        sg_bsx=jnp.repeat(sg_bs[l].T, W_GRP // SG_HEADS, axis=1),
        ta_w_out=jnp.concatenate([ta_w_out[l], jnp.zeros((128 - TA_DIM, W_GRP), F32)], axis=0).astype(BF16),
        pl_wbd=jax.scipy.linalg.block_diag(*[pl_w[l, g] for g in range(len(PL_WINDOWS))]).astype(BF16),
        pl_scale=pl_scale[l][None, :],
        cv_dw_w=cv_dw_w[l], cv_dw_b=cv_dw_b[l][None, :], cv_ng=cv_norm_gain[l][None, :],
        cv_pw_w=cv_pw_w[l].astype(BF16), cv_pw_b=cv_pw_b[l][None, :],
        ff_dw=ff_dw_w[l],
    )
    return p


def _smat():
    idx = np.arange(IN_TILE) // HEAD_DIM
    return jnp.asarray((idx[:, None] == idx[None, :]).astype(np.float32)).astype(BF16)


def _mixers(p, z, na_o, ta_o, *, tpb):
    sg_o = _sg(z, ta_o, p["sg_gain"], p["sg_wcat"], p["sg_bsx"], p["ta_w_out"], tpb=tpb)
    pl_o = _pool(z, p["pl_wbd"], p["pl_scale"], tpb=tpb)
    cv_o = _conv(z, p["cv_dw_w"], p["cv_dw_b"], p["cv_ng"], p["cv_pw_w"], p["cv_pw_b"], tpb=tpb)
    return [na_o, sg_o, pl_o, cv_o]


def _tail(p, stacks, layer, ys, x2d, mods, *, tpb):
    w_out, ff_w_up, ff_wd = stacks
    x1, hn = _outproj(ys, w_out, layer, x2d, mods[2], mods[3], mods[4], tpb=tpb)
    return _ffn(hn, x1, mods[5], ff_w_up, ff_wd, layer, p["ff_dw"], tpb=tpb)


def kernel(x, c, ctx, c_ctx, w_mod, b_mod, w_in, na_q_gain, na_k_gain, na_rpb, sg_v_gain, sg_ws, sg_bs,
           ta_q_gain, ta_k_gain, ta_w_out, pl_w, pl_scale, cv_dw_w, cv_dw_b, cv_norm_gain, cv_pw_w, cv_pw_b,
           w_out, ff_w_up, ff_dw_w, ff_w_down):
    nb, seq, d = x.shape
    lc = ctx.shape[1]
    depth = w_mod.shape[0]
    mods_all = _mod_vectors(c, c_ctx, w_mod, b_mod).reshape(depth, 8, 6, d)
    smat = _smat()
    stacks = (w_out.astype(BF16), ff_w_up.astype(BF16), ff_w_down.astype(BF16))
    w_in_r = _layout_w_in(w_in)
    cos_x, sin_x = _rope_tables(seq)
    cos_c, sin_c = jnp.ones((lc, 128), F32), jnp.zeros((lc, 128), F32)

    x2d = x.reshape(nb * seq, d)
    xc2d = ctx.reshape(nb * lc, d)
    for l in range(depth):
        last = l == depth - 1
        p = _layer_params(l, w_in, na_q_gain, na_k_gain, na_rpb, sg_v_gain, sg_ws, sg_bs, ta_q_gain, ta_k_gain,
                          ta_w_out, pl_w, pl_scale, cv_dw_w, cv_dw_b, cv_norm_gain, cv_pw_w, cv_pw_b, w_out,
                          ff_w_up, ff_dw_w, ff_w_down)
        mx = [mods_all[l, :nb, k][:, None, :] for k in range(6)]
        mc = [jnp.broadcast_to(mods_all[l, nb, k][None, None, :], (nb, 1, d)) for k in range(6)]
        zc = _inproj(xc2d, mc[0], mc[1], w_in_r, l, smat, p["gq"], p["gk"], p["gt"], cos_c, sin_c, tpb=lc)
        z = _inproj(x2d, mx[0], mx[1], w_in_r, l, smat, p["gq"], p["gk"], p["gt"], cos_x, sin_x, tpb=seq)
        na_o = _na_x(z, zc, p["na_bias"], nb=nb, seq=seq, lc=lc)
        ta_o = _ta(z, zc, nb=nb, lq=seq, le=lc)
        ys = _mixers(p, z, na_o, ta_o, tpb=seq)
        if not last:
            na_oc = _na_ctx(zc, nb=nb, lc=lc)
            ta_oc = _ta(zc, None, nb=nb, lq=lc, le=0)
            ysc = _mixers(p, zc, na_oc, ta_oc, tpb=lc)
            xc2d = _tail(p, stacks, l, ysc, xc2d, mc, tpb=lc)
        x2d = _tail(p, stacks, l, ys, x2d, mx, tpb=seq)
    return x2d.reshape(nb, seq, d)
```

```python
import functools

import numpy as np
import jax
import jax.numpy as jnp
from jax import lax
from jax.experimental import pallas as pl
from jax.experimental.pallas import tpu as pltpu

F32 = jnp.float32
BF16 = jnp.bfloat16

D_MODEL = 2048
GRID_W = 64
HEAD_DIM = 64
W_GRP = D_MODEL // 4
NA_HEADS = W_GRP // HEAD_DIM
NA_WIN_R = 8
NA_WIN_C = 16
NA_QROWS = 4
NA_BAND = NA_WIN_R + NA_QROWS
NA_SUB = 8
LOG2E = 1.4426950408889634
SG_HEADS = 8
SG_CHUNK = 128
TA_DIM = 64
PL_WINDOWS = (2, 4, 8, 16)
CV_KERNEL = 31
FF_KERNEL = 3
ROPE_BASE = 10000.0
EPS = 1e-6
HALO = 16
SUBLANES = 8
IN_TILE = 512
N_IN_TILES = 9
Z_COLS = 8 * IN_TILE
ZB_NA_Q, ZB_NA_K, ZB_NA_V = 0, 4, 8
ZB_TA_Q, ZB_TA_K, ZB_TA_V = 28, 29, 30
ZT_SG_U, ZT_SG_V, ZT_POOL, ZT_CONV, ZT_TA = 3, 4, 5, 6, 7
NEG_BIG = -1e30
NT_DIMS = (((1,), (1,)), ((), ()))
VMEM_LIMIT_MB = 56


def _cparams(sem, vmem_mb=VMEM_LIMIT_MB):
    return pltpu.CompilerParams(dimension_semantics=sem, vmem_limit_bytes=vmem_mb * 1024 * 1024)


def _dot(a, b):
    return jnp.dot(a, b, preferred_element_type=F32)


def _dot_nt(a, b):
    return lax.dot_general(a, b, NT_DIMS, preferred_element_type=F32)


def _mod_kernel(c_ref, w_ref, b_ref, o_ref):
    cv = c_ref[...]
    s = cv * jax.nn.sigmoid(cv)
    o_ref[0] = _dot(s.astype(BF16), w_ref[0].astype(BF16)) + b_ref[0]


def _mod_vectors(c, c_ctx, w_mod, b_mod):
    depth, d, n = w_mod.shape
    nb = c.shape[0]
    assert nb + 1 <= 8
    cv = jnp.zeros((8, d), F32).at[:nb].set(c).at[nb].set(c_ctx)
    tn = 1024
    return pl.pallas_call(
        _mod_kernel,
        grid=(depth, n // tn),
        in_specs=[
            pl.BlockSpec((8, d), lambda l, k: (0, 0)),
            pl.BlockSpec((1, d, tn), lambda l, k: (l, 0, k)),
            pl.BlockSpec((1, 1, tn), lambda l, k: (l, 0, k)),
        ],
        out_specs=pl.BlockSpec((1, 8, tn), lambda l, k: (l, 0, k)),
        out_shape=jax.ShapeDtypeStruct((depth, 8, n), F32),
        compiler_params=_cparams(("parallel", "parallel"), 40),
        name="mod_vectors",
    )(cv, w_mod, b_mod.reshape(depth, 1, n))


def _rope(t, cos, sin):
    lane = lax.broadcasted_iota(jnp.int32, t.shape, 1)
    partner = jnp.where((lane % 32) < 16, pltpu.roll(t, 128 - 16, 1), pltpu.roll(t, 16, 1))
    return t * cos + partner * sin


def _inproj_kernel(x_ref, sh_ref, sc_ref, w_ref, smat_ref, gq_ref, gk_ref, gt_ref, cos_ref, sin_ref,
                   z_ref, xn_scr):
    half = x_ref.shape[0] // 2
    for r in range(2):
        _inproj_rows(slice(r * half, (r + 1) * half), x_ref, sh_ref, sc_ref, w_ref, smat_ref, gq_ref, gk_ref,
                     gt_ref, cos_ref, sin_ref, z_ref, xn_scr)


def _inproj_rows(rs, x_ref, sh_ref, sc_ref, w_ref, smat_ref, gq_ref, gk_ref, gt_ref, cos_ref, sin_ref,
                 z_ref, xn_scr):
    x = x_ref[rs, :]
    ms = jnp.mean(x * x, axis=-1, keepdims=True)
    xn_scr[rs, :] = (x * lax.rsqrt(ms + EPS) * (1.0 + sc_ref[0]) + sh_ref[0]).astype(BF16)

    def proj(j):
        return _dot(xn_scr[rs, :], w_ref[:, j * IN_TILE:(j + 1) * IN_TILE])

    def put(t, val):
        z_ref[rs, t * IN_TILE:(t + 1) * IN_TILE] = val.astype(BF16)

    def head_rms(a, gain):
        sq = a * a
        hi = sq.astype(BF16)
        lo = (sq - hi.astype(F32)).astype(BF16)
        ss = _dot(hi, smat_ref[...]) + _dot(lo, smat_ref[...])
        return a * lax.rsqrt(ss * (1.0 / HEAD_DIM) + EPS) * gain

    acc = proj(8)

    def nrm(t, g):
        ms_t = jnp.sum(t * t, axis=-1, keepdims=True) * (1.0 / TA_DIM)
        return t * lax.rsqrt(ms_t + EPS) * g
    cos = cos_ref[rs, :]
    sin = sin_ref[rs, :]
    tq = _rope(nrm(acc[:, 0:128], gt_ref[:, 0:128]), cos, sin) * (TA_DIM ** -0.5 * LOG2E)
    tk = _rope(nrm(acc[:, 128:256], gt_ref[:, 128:256]), cos, sin)
    tv = acc[:, 256:384]
    lane = lax.broadcasted_iota(jnp.int32, tv.shape, 1)
    c0 = ZT_TA * IN_TILE
    z_ref[rs, c0:c0 + 128] = tq.astype(BF16)
    z_ref[rs, c0 + 128:c0 + 256] = tk.astype(BF16)
    z_ref[rs, c0 + 256:c0 + 384] = jnp.where(lane == TA_DIM, 1.0, tv).astype(BF16)
    z_ref[rs, c0 + 384:c0 + 512] = acc[:, 384:512].astype(BF16)

    put(0, head_rms(proj(0), gq_ref[...]) * (HEAD_DIM ** -0.5 * LOG2E))
    put(1, head_rms(proj(1), gk_ref[...]))
    put(ZT_SG_U, jax.nn.gelu(proj(3)))
    put(ZT_SG_V, jax.nn.gelu(proj(4)))
    put(ZT_CONV, proj(6) * jax.nn.sigmoid(proj(7)))
    put(2, proj(2))
    put(ZT_POOL, proj(5))


def _inproj(x2d, sh, sc, w_r, layer, smat, gq, gk, gt, cos_t, sin_t, *, tpb):
    t, d = x2d.shape
    tm = min(512, tpb)
    tiles_pb = tpb // tm
    const = lambda i: (0, 0)
    return pl.pallas_call(
        _inproj_kernel,
        grid=(t // tm,),
        in_specs=[
            pl.BlockSpec((tm, d), lambda i: (i, 0)),
            pl.BlockSpec((1, 1, d), lambda i: (i // tiles_pb, 0, 0)),
            pl.BlockSpec((1, 1, d), lambda i: (i // tiles_pb, 0, 0)),
            pl.BlockSpec((None, d, N_IN_TILES * IN_TILE), lambda i: (layer, 0, 0),
                         pipeline_mode=pl.Buffered(1)),
            pl.BlockSpec((IN_TILE, IN_TILE), const),
            pl.BlockSpec((1, IN_TILE), const),
            pl.BlockSpec((1, IN_TILE), const),
            pl.BlockSpec((1, 256), const),
            pl.BlockSpec((tm, 128), lambda i: (i % tiles_pb, 0)),
            pl.BlockSpec((tm, 128), lambda i: (i % tiles_pb, 0)),
        ],
        out_specs=pl.BlockSpec((tm, Z_COLS), lambda i: (i, 0)),
        out_shape=jax.ShapeDtypeStruct((t, Z_COLS), BF16),
        scratch_shapes=[pltpu.VMEM((tm, d), BF16)],
        compiler_params=_cparams(("parallel",)),
        name="in_proj",
    )(x2d, sh, sc, w_r, smat, gq, gk, gt, cos_t, sin_t)


def _stack_heads(q):
    lane = lax.broadcasted_iota(jnp.int32, q.shape, 1)
    zero = jnp.zeros_like(q)
    return jnp.concatenate([jnp.where(lane < HEAD_DIM, q, zero), jnp.where(lane >= HEAD_DIM, q, zero)], axis=0)


def _unstack_heads(o2):
    m = o2.shape[0] // 2
    lane = lax.broadcasted_iota(jnp.int32, (m, 128), 1)
    return jnp.where(lane < HEAD_DIM, o2[:m], o2[m:])


def _na_x_kernel(q_ref, k_ref, v_ref, kc_ref, vc_ref, b_ref, o_ref, *, rows):
    tq = NA_QROWS * GRID_W
    nkeys = NA_BAND * GRID_W
    nrb = rows // NA_QROWS
    kc = kc_ref[...]
    vc = vc_ref[...]
    for sub in range(NA_SUB):
        rb = pl.program_id(2) * NA_SUB + sub
        rs = jnp.clip(rb * NA_QROWS - NA_WIN_R // 2, 0, rows - NA_BAND)
        start = pl.multiple_of(rs * GRID_W, GRID_W)
        case = jnp.where(rb == 0, 0, jnp.where(rb == nrb - 1, 2, 1))
        kw = k_ref[pl.ds(start, nkeys), :]
        vw = v_ref[pl.ds(start, nkeys), :]
        q2 = _stack_heads(q_ref[sub * tq:(sub + 1) * tq, :])
        s1 = _dot_nt(q2, kw) + b_ref[case, 0]
        s2 = _dot_nt(q2, kc)
        m = jnp.maximum(jnp.max(s1, axis=-1, keepdims=True), jnp.max(s2, axis=-1, keepdims=True))
        p1 = jnp.exp2(s1 - m)
        p2 = jnp.exp2(s2 - m)
        l = jnp.sum(p1, axis=-1, keepdims=True) + jnp.sum(p2, axis=-1, keepdims=True)
        o2 = _dot(p1.astype(BF16), vw) + _dot(p2.astype(BF16), vc)
        o_ref[sub * tq:(sub + 1) * tq, :] = _unstack_heads(o2 / l).astype(BF16)


def _na_x(z, zc, bias, *, nb, seq, lc):
    rows = seq // GRID_W
    assert rows % (NA_QROWS * NA_SUB) == 0 and rows >= NA_BAND
    nsteps = rows // (NA_QROWS * NA_SUB)
    tq = NA_QROWS * GRID_W
    return pl.pallas_call(
        functools.partial(_na_x_kernel, rows=rows),
        grid=(nb, NA_HEADS // 2, nsteps),
        in_specs=[
            pl.BlockSpec((NA_SUB * tq, 128), lambda b, hp, st: (b * nsteps + st, ZB_NA_Q + hp)),
            pl.BlockSpec((seq, 128), lambda b, hp, st: (b, ZB_NA_K + hp)),
            pl.BlockSpec((seq, 128), lambda b, hp, st: (b, ZB_NA_V + hp)),
            pl.BlockSpec((lc, 128), lambda b, hp, st: (b, ZB_NA_K + hp)),
            pl.BlockSpec((lc, 128), lambda b, hp, st: (b, ZB_NA_V + hp)),
            pl.BlockSpec((3, 1, 2 * tq, NA_BAND * GRID_W), lambda b, hp, st: (0, hp, 0, 0)),
        ],
        out_specs=pl.BlockSpec((NA_SUB * tq, 128), lambda b, hp, st: (b * nsteps + st, hp)),
        out_shape=jax.ShapeDtypeStruct((nb * seq, W_GRP), BF16),
        compiler_params=_cparams(("parallel", "parallel", "arbitrary")),
        name="na_attn",
    )(z, z, z, zc, zc, bias)


def _bias_table_kernel(v_ref, o_ref):
    qc = lax.broadcasted_iota(jnp.int32, (GRID_W, 128), 0)
    kc = lax.broadcasted_iota(jnp.int32, (GRID_W, 128), 1) % GRID_W
    c0 = jnp.clip(qc - NA_WIN_C // 2, 0, GRID_W - NA_WIN_C)
    col_ok = jnp.logical_and(kc >= c0, kc < c0 + NA_WIN_C)
    npairs = NA_BAND // 2
    for hl in range(2):
        for qr in range(NA_QROWS):
            for w in range(npairs):
                i = qr * npairs + w
                v = jnp.broadcast_to(v_ref[0, hl, i:i + 1, :], (GRID_W, 128))
                x = pltpu.roll(v, 0, 1, stride=1, stride_axis=0)
                r0 = (hl * NA_QROWS + qr) * GRID_W
                o_ref[0, 0, r0:r0 + GRID_W, w * 128:(w + 1) * 128] = jnp.where(col_ok, x, NEG_BIG)


def _na_bias_table(rpb):
    h = rpb.shape[0]
    nq = NA_QROWS * GRID_W
    nk = NA_BAND * GRID_W
    nrel = 2 * NA_WIN_C - 1
    npairs = NA_BAND // 2
    d = np.arange(nrel) - (NA_WIN_C - 1)
    place = np.zeros((2, nrel, 128), np.float32)
    place[0, np.arange(nrel), d % 128] = 1.0
    place[1, np.arange(nrel), GRID_W + d] = 1.0
    support = place.sum(axis=1)
    qr, wr = np.arange(NA_QROWS), np.arange(NA_BAND)
    half = NA_WIN_R // 2
    cases = ((0, np.zeros(NA_QROWS, int)), (half, np.arange(NA_QROWS)), (NA_WIN_R, np.full(NA_QROWS, half)))
    ridx, row_ok = [], []
    for dq, r0 in cases:
        row_ok.append((wr[None, :] >= r0[:, None]) & (wr[None, :] < r0[:, None] + NA_WIN_R))
        ridx.append(np.clip(wr[None, :] - (dq + qr[:, None]) + NA_WIN_R - 1, 0, 2 * NA_WIN_R - 2))
    ridx = np.stack(ridx).reshape(3, NA_QROWS, npairs, 2)
    row_ok = np.stack(row_ok).reshape(3, NA_QROWS, npairs, 2)
    rows = rpb[:, ridx, :] * LOG2E
    vec = jnp.einsum("hbqwpc,pcl->bhqwpl", rows, jnp.asarray(place), precision=lax.Precision.HIGHEST)
    masked = jnp.asarray(NEG_BIG * support)
    vec = jnp.where(jnp.asarray(row_ok)[:, None, :, :, :, None], vec, masked).sum(axis=4)
    vec = vec.reshape(3, h, NA_QROWS * npairs, 128)
    return pl.pallas_call(
        _bias_table_kernel,
        grid=(3, h // 2),
        in_specs=[pl.BlockSpec((1, 2, NA_QROWS * npairs, 128), lambda b, hp: (b, hp, 0, 0))],
        out_specs=pl.BlockSpec((1, 1, 2 * nq, nk), lambda b, hp: (b, hp, 0, 0)),
        out_shape=jax.ShapeDtypeStruct((3, h // 2, 2 * nq, nk), F32),
        compiler_params=_cparams(("parallel", "parallel")),
        name="na_bias_table",
    )(vec)


def _na_ctx_kernel(q_ref, k_ref, v_ref, o_ref):
    q2 = _stack_heads(q_ref[...])
    s = _dot_nt(q2, k_ref[...])
    m = jnp.max(s, axis=-1, keepdims=True)
    p = jnp.exp2(s - m)
    l = jnp.sum(p, axis=-1, keepdims=True)
    o2 = _dot(p.astype(BF16), v_ref[...])
    o_ref[...] = _unstack_heads(o2 / l).astype(BF16)


def _na_ctx(zc, *, nb, lc):
    return pl.pallas_call(
        _na_ctx_kernel,
        grid=(nb, NA_HEADS // 2),
        in_specs=[
            pl.BlockSpec((lc, 128), lambda b, hp: (b, ZB_NA_Q + hp)),
            pl.BlockSpec((lc, 128), lambda b, hp: (b, ZB_NA_K + hp)),
            pl.BlockSpec((lc, 128), lambda b, hp: (b, ZB_NA_V + hp)),
        ],
        out_specs=pl.BlockSpec((lc, 128), lambda b, hp: (b, hp)),
        out_shape=jax.ShapeDtypeStruct((nb * lc, W_GRP), BF16),
        compiler_params=_cparams(("parallel", "parallel")),
        name="na_attn_ctx",
    )(zc, zc, zc)


def _ta_kernel(*refs, tk, nk, has_extra):
    if has_extra:
        q_ref, k_ref, v_ref, ke_ref, ve_ref, o_ref = refs
    else:
        q_ref, k_ref, v_ref, o_ref = refs
    q = q_ref[...]
    tq = q.shape[0]

    def step(kb, vb, carry):
        m, acc = carry
        s = _dot_nt(q, kb)
        mn = jnp.maximum(m, jnp.max(s, axis=-1, keepdims=True))
        p = jnp.exp2(s - mn).astype(BF16)
        acc = jnp.exp2(m - mn) * acc + _dot(p, vb)
        return mn, acc

    def body(c, carry):
        st = pl.multiple_of(c * tk, tk)
        return step(k_ref[pl.ds(st, tk), :], v_ref[pl.ds(st, tk), :], carry)

    carry = (jnp.full((tq, 1), NEG_BIG, F32), jnp.zeros((tq, 128), F32))
    carry = lax.fori_loop(0, nk, body, carry, unroll=8 if nk % 8 == 0 else 1)
    if has_extra:
        carry = step(ke_ref[...], ve_ref[...], carry)
    _, acc = carry
    lane = lax.broadcasted_iota(jnp.int32, acc.shape, 1)
    l = jnp.sum(jnp.where(lane == TA_DIM, acc, 0.0), axis=-1, keepdims=True)
    o_ref[...] = jnp.where(lane < TA_DIM, acc / l, 0.0).astype(BF16)


def _ta(zq, zextra, *, nb, lq, le):
    tq = min(1024, lq)
    tk = min(1024, lq)
    nqb = lq // tq
    has_extra = zextra is not None
    in_specs = [
        pl.BlockSpec((tq, 128), lambda b, qi: (b * nqb + qi, ZB_TA_Q)),
        pl.BlockSpec((lq, 128), lambda b, qi: (b, ZB_TA_K)),
        pl.BlockSpec((lq, 128), lambda b, qi: (b, ZB_TA_V)),
    ]
    args = [zq, zq, zq]
    if has_extra:
        in_specs += [
            pl.BlockSpec((le, 128), lambda b, qi: (b, ZB_TA_K)),
            pl.BlockSpec((le, 128), lambda b, qi: (b, ZB_TA_V)),
        ]
        args += [zextra, zextra]
    return pl.pallas_call(
        functools.partial(_ta_kernel, tk=tk, nk=lq // tk, has_extra=has_extra),
        grid=(nb, nqb),
        in_specs=in_specs,
        out_specs=pl.BlockSpec((tq, 128), lambda b, qi: (b * nqb + qi, 0)),
        out_shape=jax.ShapeDtypeStruct((nb * lq, 128), BF16),
        compiler_params=_cparams(("parallel", "arbitrary")),
        name="ta_attn",
    )(*args)


def _sg_kernel(u_ref, gv_ref, tao_ref, gain_ref, wcat_ref, bsx_ref, two_ref, o_ref, v_scr):
    tm = u_ref.shape[0]
    gv = gv_ref[...].astype(F32)
    ms = jnp.mean(gv * gv, axis=-1, keepdims=True)
    v_scr[...] = (gv * lax.rsqrt(ms + EPS) * gain_ref[...]).astype(BF16)
    lane = lax.broadcasted_iota(jnp.int32, (SG_CHUNK, 128), 1)
    zero = jnp.zeros((SG_CHUNK, 128), BF16)

    def chunk(c, carry):
        r0 = pl.multiple_of(c * SG_CHUNK, SG_CHUNK)
        rows = pl.ds(r0, SG_CHUNK)
        outs = []
        for p in range(SG_HEADS // 2):
            vs = v_scr[rows, p * 128:(p + 1) * 128]
            rhs = jnp.concatenate([jnp.where(lane < 64, vs, zero), jnp.where(lane >= 64, vs, zero)], axis=0)
            outs.append(_dot(wcat_ref[p], rhs))
        s = jnp.concatenate(outs, axis=1) + bsx_ref[...] + _dot(tao_ref[rows, :], two_ref[...])
        o_ref[rows, :] = (u_ref[rows, :].astype(F32) * s).astype(BF16)
        return carry

    lax.fori_loop(0, tm // SG_CHUNK, chunk, 0)


def _sg(z, tao, gain, wcat, bsx, two, *, tpb):
    t = z.shape[0]
    tm = min(1024, tpb)
    const2 = lambda i: (0, 0)
    return pl.pallas_call(
        _sg_kernel,
        grid=(t // tm,),
        in_specs=[
            pl.BlockSpec((tm, W_GRP), lambda i: (i, ZT_SG_U)),
            pl.BlockSpec((tm, W_GRP), lambda i: (i, ZT_SG_V)),
            pl.BlockSpec((tm, 128), lambda i: (i, 0)),
            pl.BlockSpec((1, W_GRP), const2),
            pl.BlockSpec((SG_HEADS // 2, SG_CHUNK, 2 * SG_CHUNK), lambda i: (0, 0, 0)),
            pl.BlockSpec((SG_CHUNK, W_GRP), const2),
            pl.BlockSpec((128, W_GRP), const2),
        ],
        out_specs=pl.BlockSpec((tm, W_GRP), lambda i: (i, 0)),
        out_shape=jax.ShapeDtypeStruct((t, W_GRP), BF16),
        scratch_shapes=[pltpu.VMEM((tm, W_GRP), BF16)],
        compiler_params=_cparams(("parallel",)),
        name="sg_unit",
    )(z, z, tao, gain, wcat, bsx, two)


def _halo_specs(tm, t, width, col):
    hb = tm // HALO
    last = t // HALO - 1
    return [
        pl.BlockSpec((tm, width), lambda i: (i, col)),
        pl.BlockSpec((HALO, width), lambda i: (jnp.maximum(i * hb - 1, 0), col)),
        pl.BlockSpec((HALO, width), lambda i: (jnp.minimum((i + 1) * hb, last), col)),
    ]


def _fill_ext(ext_scr, main_ref, prev_ref, next_ref, it, tiles_pb):
    tm = main_ref.shape[0]
    zero = jnp.zeros(prev_ref.shape, F32)
    ext_scr[0:HALO, :] = jnp.where(it == 0, zero, prev_ref[...].astype(F32))
    ext_scr[HALO:HALO + tm, :] = main_ref[...].astype(F32)
    ext_scr[HALO + tm:2 * HALO + tm, :] = jnp.where(it == tiles_pb - 1, zero, next_ref[...].astype(F32))


def _pool_kernel(z_ref, zp_ref, zn_ref, wbd_ref, scale_ref, o_ref, ext_scr, d_scr, *, tiles_pb, seq):
    tm = z_ref.shape[0]
    it = pl.program_id(0) % tiles_pb
    _fill_ext(ext_scr, z_ref, zp_ref, zn_ref, it, tiles_pb)
    rc = 128

    for c in range(tm // rc):
        r0 = c * rc
        pos = it * tm + r0 + lax.broadcasted_iota(jnp.int32, (rc, 1), 0)
        for g, w in enumerate(PL_WINDOWS):
            cols = slice(g * 128, (g + 1) * 128)
            lo = r0 + HALO - w // 2
            acc = ext_scr[lo:lo + rc, cols]
            for s in range(1, w):
                acc = acc + ext_scr[lo + s:lo + s + rc, cols]
            cnt = (jnp.minimum(pos + w // 2, seq) - jnp.maximum(pos - w // 2, 0)).astype(F32)
            d = acc / cnt - ext_scr[r0 + HALO:r0 + HALO + rc, cols]
            d_scr[r0:r0 + rc, cols] = d.astype(BF16)
    o_ref[...] = (_dot(d_scr[...], wbd_ref[...]) * scale_ref[...]).astype(BF16)


def _pool(z, wbd, scale, *, tpb):
    t = z.shape[0]
    tm = min(1024, tpb)
    const2 = lambda i: (0, 0)
    return pl.pallas_call(
        functools.partial(_pool_kernel, tiles_pb=tpb // tm, seq=tpb),
        grid=(t // tm,),
        in_specs=_halo_specs(tm, t, W_GRP, ZT_POOL) + [
            pl.BlockSpec((W_GRP, W_GRP), const2),
            pl.BlockSpec((1, W_GRP), const2),
        ],
        out_specs=pl.BlockSpec((tm, W_GRP), lambda i: (i, 0)),
        out_shape=jax.ShapeDtypeStruct((t, W_GRP), BF16),
        scratch_shapes=[pltpu.VMEM((tm + 2 * HALO, W_GRP), F32), pltpu.VMEM((tm, W_GRP), BF16)],
        compiler_params=_cparams(("parallel",)),
        name="pool_mixer",
    )(z, z, z, wbd, scale)


def _conv_kernel(h_ref, hp_ref, hn_ref, dww_ref, dwb_ref, ng_ref, pww_ref, pwb_ref, o_ref, ext_scr, sh_scr,
                 a_scr, *, tiles_pb):
    tm = h_ref.shape[0]
    it = pl.program_id(0) % tiles_pb
    _fill_ext(ext_scr, h_ref, hp_ref, hn_ref, it, tiles_pb)
    rc = 64
    base = HALO - (CV_KERNEL - 1) // 2
    nsh = sh_scr.shape[1]
    for s in range(1, SUBLANES):
        sh_scr[s - 1] = ext_scr[s:s + nsh, :]

    for c in range(tm // rc):
        r0 = c * rc
        acc = jnp.broadcast_to(dwb_ref[...], (rc, W_GRP))
        for k in range(CV_KERNEL):
            a, s = divmod(base + k, SUBLANES)
            lo = r0 + a * SUBLANES
            rows = ext_scr[lo:lo + rc, :] if s == 0 else sh_scr[s - 1, lo:lo + rc, :]
            acc = acc + dww_ref[k:k + 1, :] * rows
        ms = jnp.mean(acc * acc, axis=-1, keepdims=True)
        hh = acc * lax.rsqrt(ms + EPS) * ng_ref[...]
        a_scr[r0:r0 + rc, :] = (hh * jax.nn.sigmoid(hh)).astype(BF16)
    o_ref[...] = (_dot(a_scr[...], pww_ref[...]) + pwb_ref[...]).astype(BF16)


def _conv(z, dww, dwb, ng, pww, pwb, *, tpb):
    t = z.shape[0]
    tm = min(512, tpb)
    const2 = lambda i: (0, 0)
    return pl.pallas_call(
        functools.partial(_conv_kernel, tiles_pb=tpb // tm),
        grid=(t // tm,),
        in_specs=_halo_specs(tm, t, W_GRP, ZT_CONV) + [
            pl.BlockSpec((CV_KERNEL, W_GRP), const2),
            pl.BlockSpec((1, W_GRP), const2),
            pl.BlockSpec((1, W_GRP), const2),
            pl.BlockSpec((W_GRP, W_GRP), const2),
            pl.BlockSpec((1, W_GRP), const2),
        ],
        out_specs=pl.BlockSpec((tm, W_GRP), lambda i: (i, 0)),
        out_shape=jax.ShapeDtypeStruct((t, W_GRP), BF16),
        scratch_shapes=[pltpu.VMEM((tm + 2 * HALO, W_GRP), F32),
                        pltpu.VMEM((SUBLANES - 1, tm + 2 * HALO - SUBLANES, W_GRP), F32),
                        pltpu.VMEM((tm, W_GRP), BF16)],
        compiler_params=_cparams(("parallel",)),
        name="conformer_conv",
    )(z, z, z, dww, dwb, ng, pww, pwb)


def _outproj_kernel(yna_ref, ysg_ref, ypl_ref, ycv_ref, w_ref, x_ref, g1_ref, sh2_ref, sc2_ref, x1_ref, hn_ref):
    tm, d = x_ref.shape
    y = jnp.concatenate([yna_ref[...], ysg_ref[...], ypl_ref[...], ycv_ref[...]], axis=1)
    nc = 512
    ssq = jnp.zeros((tm, 1), F32)
    for n in range(d // nc):
        cs = slice(n * nc, (n + 1) * nc)
        x1 = x_ref[:, cs] + g1_ref[0][:, cs] * _dot(y, w_ref[:, cs])
        x1_ref[:, cs] = x1
        ssq = ssq + jnp.sum(x1 * x1, axis=-1, keepdims=True)
    r = lax.rsqrt(ssq * (1.0 / d) + EPS)
    for n in range(d // nc):
        cs = slice(n * nc, (n + 1) * nc)
        hn_ref[:, cs] = (x1_ref[:, cs] * r * (1.0 + sc2_ref[0][:, cs]) + sh2_ref[0][:, cs]).astype(BF16)


def _outproj(ys, w_out, layer, x2d, g1, sh2, sc2, *, tpb):
    t, d = x2d.shape
    tm = min(512, tpb)
    tiles_pb = tpb // tm
    row = lambda i: (i // tiles_pb, 0, 0)
    return pl.pallas_call(
        _outproj_kernel,
        grid=(t // tm,),
        in_specs=[pl.BlockSpec((tm, W_GRP), lambda i: (i, 0))] * 4 + [
            pl.BlockSpec((None, d, d), lambda i: (layer, 0, 0)),
            pl.BlockSpec((tm, d), lambda i: (i, 0)),
            pl.BlockSpec((1, 1, d), row),
            pl.BlockSpec((1, 1, d), row),
            pl.BlockSpec((1, 1, d), row),
        ],
        out_specs=[pl.BlockSpec((tm, d), lambda i: (i, 0)), pl.BlockSpec((tm, d), lambda i: (i, 0))],
        out_shape=[jax.ShapeDtypeStruct((t, d), F32), jax.ShapeDtypeStruct((t, d), BF16)],
        compiler_params=_cparams(("parallel",)),
        name="out_proj",
    )(*ys, w_out, x2d, g1, sh2, sc2)


def _ffn_kernel(hn_ref, hp_ref, hx_ref, x1_ref, g2_ref, wg_ref, wv_ref, wd_ref, dw_ref, o_ref, ext_scr, g_scr,
                *, tiles_pb, nx):
    tm, d = hn_ref.shape
    j = pl.program_id(1)
    it = pl.program_id(0) % tiles_pb

    @pl.when(j == 0)
    def _():
        zero = jnp.zeros((HALO, d), BF16)
        ext_scr[0:HALO, :] = jnp.where(it == 0, zero, hp_ref[...])
        ext_scr[HALO:HALO + tm, :] = hn_ref[...]
        ext_scr[HALO + tm:2 * HALO + tm, :] = jnp.where(it == tiles_pb - 1, zero, hx_ref[...])
        o_ref[...] = jnp.zeros(o_ref.shape, F32)

    g_scr[...] = _dot(ext_scr[...], wg_ref[...])
    val = _dot(hn_ref[...], wv_ref[...])
    c = (dw_ref[0:1, :] * g_scr[HALO - 1:HALO - 1 + tm, :] + dw_ref[1:2, :] * g_scr[HALO:HALO + tm, :]
         + dw_ref[2:3, :] * g_scr[HALO + 1:HALO + 1 + tm, :])
    a = (c * jax.nn.sigmoid(c) * val).astype(BF16)
    nc = 512
    for n in range(d // nc):
        cs = slice(n * nc, (n + 1) * nc)
        o_ref[:, cs] += g2_ref[0][:, cs] * _dot(a, wd_ref[:, cs])

    xc = d // nx
    for s in range(nx):
        @pl.when(j == s)
        def _(s=s):
            o_ref[:, s * xc:(s + 1) * xc] += x1_ref[...]


def _ffn(hn, x1, g2, w_up, wd, layer, dw, *, tpb):
    t, d = x1.shape
    ff = wd.shape[1]
    tm = min(1024, tpb)
    tiles_pb = tpb // tm
    tf = 512
    nj = ff // tf
    nx = 8
    assert nj >= nx
    hb = tm // HALO
    last = t // HALO - 1
    return pl.pallas_call(
        functools.partial(_ffn_kernel, tiles_pb=tiles_pb, nx=nx),
        grid=(t // tm, nj),
        in_specs=[
            pl.BlockSpec((tm, d), lambda i, j: (i, 0)),
            pl.BlockSpec((HALO, d), lambda i, j: (jnp.maximum(i * hb - 1, 0), 0)),
            pl.BlockSpec((HALO, d), lambda i, j: (jnp.minimum((i + 1) * hb, last), 0)),
            pl.BlockSpec((tm, d // nx), lambda i, j: (i, jnp.minimum(j, nx - 1))),
            pl.BlockSpec((1, 1, d), lambda i, j: (i // tiles_pb, 0, 0)),
            pl.BlockSpec((None, d, tf), lambda i, j: (layer, 0, j)),
            pl.BlockSpec((None, d, tf), lambda i, j: (layer, 0, nj + j)),
            pl.BlockSpec((None, tf, d), lambda i, j: (layer, j, 0)),
            pl.BlockSpec((FF_KERNEL, tf), lambda i, j: (0, j)),
        ],
        out_specs=pl.BlockSpec((tm, d), lambda i, j: (i, 0)),
        out_shape=jax.ShapeDtypeStruct((t, d), F32),
        scratch_shapes=[pltpu.VMEM((tm + 2 * HALO, d), BF16), pltpu.VMEM((tm + 2 * HALO, tf), F32)],
        compiler_params=_cparams(("parallel", "arbitrary")),
        name="conv_glu_ffn",
    )(hn, hn, hn, x1, g2, w_up, w_up, wd, dw)


def _rope_tables(seq):
    half = TA_DIM // 2
    freqs = ROPE_BASE ** (-np.arange(0, half, 2, dtype=np.float32) / half)
    pos = np.arange(seq)
    lane = np.arange(TA_DIM)
    p = np.where(lane[None, :] < half, (pos // GRID_W)[:, None], (pos % GRID_W)[:, None]).astype(np.float32)
    m = lane % half
    ang = jnp.asarray(p * freqs[m % (half // 2)][None, :])
    sign = jnp.asarray(np.where(m < half // 2, -1.0, 1.0).astype(np.float32))[None, :]
    cos = jnp.concatenate([jnp.cos(ang), jnp.ones((seq, 128 - TA_DIM), F32)], axis=1)
    sin = jnp.concatenate([jnp.sin(ang) * sign, jnp.zeros((seq, 128 - TA_DIM), F32)], axis=1)
    return cos, sin


def _layout_w_in(w):
    lead = w.shape[:-1]
    g = W_GRP
    w = w.astype(BF16)
    o = 5 * g
    tq, tk, tv = (w[..., o + i * TA_DIM:o + (i + 1) * TA_DIM] for i in range(3))
    z64 = jnp.zeros(lead + (128 - TA_DIM,), BF16)
    tail = jnp.zeros(lead + (IN_TILE - 3 * 128 + 128 - TA_DIM,), BF16)
    return jnp.concatenate([w[..., :o], w[..., o + 3 * TA_DIM:], tq, z64, tk, z64, tv, tail], axis=-1)


def _layer_params(l, w_in, na_q_gain, na_k_gain, na_rpb, sg_v_gain, sg_ws, sg_bs, ta_q_gain, ta_k_gain,
                  ta_w_out, pl_w, pl_scale, cv_dw_w, cv_dw_b, cv_norm_gain, cv_pw_w, cv_pw_b, w_out,
                  ff_w_up, ff_dw_w, ff_w_down):
    ones64 = jnp.ones((128 - TA_DIM,), F32)
    ws = sg_ws[l].astype(BF16)
    ff = ff_w_down.shape[1]
    p = dict(
        gq=jnp.tile(na_q_gain[l], NA_HEADS)[None, :],
        gk=jnp.tile(na_k_gain[l], NA_HEADS)[None, :],
        gt=jnp.concatenate([ta_q_gain[l], ones64, ta_k_gain[l], ones64])[None, :],
        na_bias=_na_bias_table(na_rpb[l]),
        sg_gain=sg_v_gain[l][None, :],
        sg_wcat=jnp.concatenate([ws[0::2], ws[1::2]], axis=2),
        sg_bsx=jnp.repeat(sg_bs[l].T, W_GRP // SG_HEADS, axis=1),
        ta_w_out=jnp.concatenate([ta_w_out[l], jnp.zeros((128 - TA_DIM, W_GRP), F32)], axis=0).astype(BF16),
        pl_wbd=jax.scipy.linalg.block_diag(*[pl_w[l, g] for g in range(len(PL_WINDOWS))]).astype(BF16),
        pl_scale=pl_scale[l][None, :],
        cv_dw_w=cv_dw_w[l], cv_dw_b=cv_dw_b[l][None, :], cv_ng=cv_norm_gain[l][None, :],
        cv_pw_w=cv_pw_w[l].astype(BF16), cv_pw_b=cv_pw_b[l][None, :],
        ff_dw=ff_dw_w[l],
    )
    return p


def _smat():
    idx = np.arange(IN_TILE) // HEAD_DIM
    return jnp.asarray((idx[:, None] == idx[None, :]).astype(np.float32)).astype(BF16)


def _mixers(p, z, na_o, ta_o, *, tpb):
    sg_o = _sg(z, ta_o, p["sg_gain"], p["sg_wcat"], p["sg_bsx"], p["ta_w_out"], tpb=tpb)
    pl_o = _pool(z, p["pl_wbd"], p["pl_scale"], tpb=tpb)
    cv_o = _conv(z, p["cv_dw_w"], p["cv_dw_b"], p["cv_ng"], p["cv_pw_w"], p["cv_pw_b"], tpb=tpb)
    return [na_o, sg_o, pl_o, cv_o]


def _tail(p, stacks, layer, ys, x2d, mods, *, tpb):
    w_out, ff_w_up, ff_wd = stacks
    x1, hn = _outproj(ys, w_out, layer, x2d, mods[2], mods[3], mods[4], tpb=tpb)
    return _ffn(hn, x1, mods[5], ff_w_up, ff_wd, layer, p["ff_dw"], tpb=tpb)


def kernel(x, c, ctx, c_ctx, w_mod, b_mod, w_in, na_q_gain, na_k_gain, na_rpb, sg_v_gain, sg_ws, sg_bs,
           ta_q_gain, ta_k_gain, ta_w_out, pl_w, pl_scale, cv_dw_w, cv_dw_b, cv_norm_gain, cv_pw_w, cv_pw_b,
           w_out, ff_w_up, ff_dw_w, ff_w_down):
    nb, seq, d = x.shape
    lc = ctx.shape[1]
    depth = w_mod.shape[0]
    mods_all = _mod_vectors(c, c_ctx, w_mod, b_mod).reshape(depth, 8, 6, d)
    smat = _smat()
    stacks = (w_out.astype(BF16), ff_w_up.astype(BF16), ff_w_down.astype(BF16))
    w_in_r = _layout_w_in(w_in)
    cos_x, sin_x = _rope_tables(seq)
    cos_c, sin_c = jnp.ones((lc, 128), F32), jnp.zeros((lc, 128), F32)

    x2d = x.reshape(nb * seq, d)
    xc2d = ctx.reshape(nb * lc, d)
    for l in range(depth):
        last = l == depth - 1
        p = _layer_params(l, w_in, na_q_gain, na_k_gain, na_rpb, sg_v_gain, sg_ws, sg_bs, ta_q_gain, ta_k_gain,
                          ta_w_out, pl_w, pl_scale, cv_dw_w, cv_dw_b, cv_norm_gain, cv_pw_w, cv_pw_b, w_out,
                          ff_w_up, ff_dw_w, ff_w_down)
        mx = [mods_all[l, :nb, k][:, None, :] for k in range(6)]
        mc = [jnp.broadcast_to(mods_all[l, nb, k][None, None, :], (nb, 1, d)) for k in range(6)]
        zc = _inproj(xc2d, mc[0], mc[1], w_in_r, l, smat, p["gq"], p["gk"], p["gt"], cos_c, sin_c, tpb=lc)
        z = _inproj(x2d, mx[0], mx[1], w_in_r, l, smat, p["gq"], p["gk"], p["gt"], cos_x, sin_x, tpb=seq)
        na_o = _na_x(z, zc, p["na_bias"], nb=nb, seq=seq, lc=lc)
        ta_o = _ta(z, zc, nb=nb, lq=seq, le=lc)
        ys = _mixers(p, z, na_o, ta_o, tpb=seq)
        if not last:
            na_oc = _na_ctx(zc, nb=nb, lc=lc)
            ta_oc = _ta(zc, None, nb=nb, lq=lc, le=0)
            ysc = _mixers(p, zc, na_oc, ta_oc, tpb=lc)
            xc2d = _tail(p, stacks, l, ysc, xc2d, mc, tpb=lc)
        x2d = _tail(p, stacks, l, ys, x2d, mx, tpb=seq)
    return x2d.reshape(nb, seq, d)
```

```python
import functools

import numpy as np
import jax
import jax.numpy as jnp
from jax import lax
from jax.experimental import pallas as pl
from jax.experimental.pallas import tpu as pltpu

F32 = jnp.float32
BF16 = jnp.bfloat16

D_MODEL = 2048
GRID_W = 64
HEAD_DIM = 64
W_GRP = D_MODEL // 4
NA_HEADS = W_GRP // HEAD_DIM
NA_WIN_R = 8
NA_WIN_C = 16
NA_QROWS = 4
NA_BAND = NA_WIN_R + NA_QROWS
NA_SUB = 8
LOG2E = 1.4426950408889634
SG_HEADS = 8
SG_CHUNK = 128
TA_DIM = 64
PL_WINDOWS = (2, 4, 8, 16)
CV_KERNEL = 31
FF_KERNEL = 3
ROPE_BASE = 10000.0
EPS = 1e-6
HALO = 16
SUBLANES = 8
IN_TILE = 512
N_IN_TILES = 9
Z_COLS = 8 * IN_TILE
ZB_NA_Q, ZB_NA_K, ZB_NA_V = 0, 4, 8
ZB_TA_Q, ZB_TA_K, ZB_TA_V = 28, 29, 30
ZT_SG_U, ZT_SG_V, ZT_POOL, ZT_CONV, ZT_TA = 3, 4, 5, 6, 7
NEG_BIG = -1e30
NT_DIMS = (((1,), (1,)), ((), ()))
VMEM_LIMIT_MB = 56


def _cparams(sem, vmem_mb=VMEM_LIMIT_MB):
    return pltpu.CompilerParams(dimension_semantics=sem, vmem_limit_bytes=vmem_mb * 1024 * 1024)


def _dot(a, b):
    return jnp.dot(a, b, preferred_element_type=F32)


def _dot_nt(a, b):
    return lax.dot_general(a, b, NT_DIMS, preferred_element_type=F32)


def _mod_kernel(c_ref, w_ref, b_ref, o_ref):
    cv = c_ref[...]
    s = cv * jax.nn.sigmoid(cv)
    o_ref[0] = _dot(s.astype(BF16), w_ref[0].astype(BF16)) + b_ref[0]


def _mod_vectors(c, c_ctx, w_mod, b_mod):
    depth, d, n = w_mod.shape
    nb = c.shape[0]
    assert nb + 1 <= 8
    cv = jnp.zeros((8, d), F32).at[:nb].set(c).at[nb].set(c_ctx)
    tn = 1024
    return pl.pallas_call(
        _mod_kernel,
        grid=(depth, n // tn),
        in_specs=[
            pl.BlockSpec((8, d), lambda l, k: (0, 0)),
            pl.BlockSpec((1, d, tn), lambda l, k: (l, 0, k)),
            pl.BlockSpec((1, 1, tn), lambda l, k: (l, 0, k)),
        ],
        out_specs=pl.BlockSpec((1, 8, tn), lambda l, k: (l, 0, k)),
        out_shape=jax.ShapeDtypeStruct((depth, 8, n), F32),
        compiler_params=_cparams(("parallel", "parallel"), 40),
        name="mod_vectors",
    )(cv, w_mod, b_mod.reshape(depth, 1, n))


def _rope(t, cos, sin):
    lane = lax.broadcasted_iota(jnp.int32, t.shape, 1)
    partner = jnp.where((lane % 32) < 16, pltpu.roll(t, 128 - 16, 1), pltpu.roll(t, 16, 1))
    return t * cos + partner * sin


def _inproj_kernel(x_ref, sh_ref, sc_ref, w_ref, smat_ref, gq_ref, gk_ref, gt_ref, cos_ref, sin_ref,
                   z_ref, xn_scr):
    half = x_ref.shape[0] // 2
    for r in range(2):
        _inproj_rows(slice(r * half, (r + 1) * half), x_ref, sh_ref, sc_ref, w_ref, smat_ref, gq_ref, gk_ref,
                     gt_ref, cos_ref, sin_ref, z_ref, xn_scr)


def _inproj_rows(rs, x_ref, sh_ref, sc_ref, w_ref, smat_ref, gq_ref, gk_ref, gt_ref, cos_ref, sin_ref,
                 z_ref, xn_scr):
    x = x_ref[rs, :]
    ms = jnp.mean(x * x, axis=-1, keepdims=True)
    xn_scr[rs, :] = (x * lax.rsqrt(ms + EPS) * (1.0 + sc_ref[0]) + sh_ref[0]).astype(BF16)

    def proj(j):
        return _dot(xn_scr[rs, :], w_ref[:, j * IN_TILE:(j + 1) * IN_TILE])

    def put(t, val):
        z_ref[rs, t * IN_TILE:(t + 1) * IN_TILE] = val.astype(BF16)

    def head_rms(a, gain):
        sq = a * a
        hi = sq.astype(BF16)
        lo = (sq - hi.astype(F32)).astype(BF16)
        ss = _dot(hi, smat_ref[...]) + _dot(lo, smat_ref[...])
        r = lax.rsqrt(ss * (1.0 / HEAD_DIM) + EPS)
        lane = lax.broadcasted_iota(jnp.int32, (a.shape[0], 128), 1)
        slabs = [jnp.where(lane < HEAD_DIM, r[:, 2 * p:2 * p + 1], r[:, 2 * p + 1:2 * p + 2])
                 for p in range(NA_HEADS // 2)]
        return a * jnp.concatenate(slabs, axis=1) * gain

    acc = proj(8)

    def nrm(t, g):
        ms_t = jnp.sum(t * t, axis=-1, keepdims=True) * (1.0 / TA_DIM)
        return t * lax.rsqrt(ms_t + EPS) * g
    cos = cos_ref[rs, :]
    sin = sin_ref[rs, :]
    tq = _rope(nrm(acc[:, 0:128], gt_ref[:, 0:128]), cos, sin) * (TA_DIM ** -0.5 * LOG2E)
    tk = _rope(nrm(acc[:, 128:256], gt_ref[:, 128:256]), cos, sin)
    tv = acc[:, 256:384]
    lane = lax.broadcasted_iota(jnp.int32, tv.shape, 1)
    c0 = ZT_TA * IN_TILE
    z_ref[rs, c0:c0 + 128] = tq.astype(BF16)
    z_ref[rs, c0 + 128:c0 + 256] = tk.astype(BF16)
    z_ref[rs, c0 + 256:c0 + 384] = jnp.where(lane == TA_DIM, 1.0, tv).astype(BF16)
    z_ref[rs, c0 + 384:c0 + 512] = acc[:, 384:512].astype(BF16)

    put(0, head_rms(proj(0), gq_ref[...]) * (HEAD_DIM ** -0.5 * LOG2E))
    put(1, head_rms(proj(1), gk_ref[...]))
    put(ZT_SG_U, jax.nn.gelu(proj(3)))
    put(ZT_SG_V, jax.nn.gelu(proj(4)))
    put(ZT_CONV, proj(6) * jax.nn.sigmoid(proj(7)))
    put(2, proj(2))
    put(ZT_POOL, proj(5))


def _inproj(x2d, sh, sc, w_r, layer, smat, gq, gk, gt, cos_t, sin_t, *, tpb):
    t, d = x2d.shape
    tm = min(512, tpb)
    tiles_pb = tpb // tm
    const = lambda i: (0, 0)
    return pl.pallas_call(
        _inproj_kernel,
        grid=(t // tm,),
        in_specs=[
            pl.BlockSpec((tm, d), lambda i: (i, 0)),
            pl.BlockSpec((1, 1, d), lambda i: (i // tiles_pb, 0, 0)),
            pl.BlockSpec((1, 1, d), lambda i: (i // tiles_pb, 0, 0)),
            pl.BlockSpec((None, d, N_IN_TILES * IN_TILE), lambda i: (layer, 0, 0),
                         pipeline_mode=pl.Buffered(1)),
            pl.BlockSpec((IN_TILE, 128), const),
            pl.BlockSpec((1, IN_TILE), const),
            pl.BlockSpec((1, IN_TILE), const),
            pl.BlockSpec((1, 256), const),
            pl.BlockSpec((tm, 128), lambda i: (i % tiles_pb, 0)),
            pl.BlockSpec((tm, 128), lambda i: (i % tiles_pb, 0)),
        ],
        out_specs=pl.BlockSpec((tm, Z_COLS), lambda i: (i, 0)),
        out_shape=jax.ShapeDtypeStruct((t, Z_COLS), BF16),
        scratch_shapes=[pltpu.VMEM((tm, d), BF16)],
        compiler_params=_cparams(("parallel",)),
        name="in_proj",
    )(x2d, sh, sc, w_r, smat, gq, gk, gt, cos_t, sin_t)


def _stack_heads(q):
    lane = lax.broadcasted_iota(jnp.int32, q.shape, 1)
    zero = jnp.zeros_like(q)
    return jnp.concatenate([jnp.where(lane < HEAD_DIM, q, zero), jnp.where(lane >= HEAD_DIM, q, zero)], axis=0)


def _unstack_heads(o2):
    m = o2.shape[0] // 2
    lane = lax.broadcasted_iota(jnp.int32, (m, 128), 1)
    return jnp.where(lane < HEAD_DIM, o2[:m], o2[m:])


def _na_x_kernel(q_ref, k_ref, v_ref, kc_ref, vc_ref, b_ref, o_ref, *, rows):
    tq = NA_QROWS * GRID_W
    nkeys = NA_BAND * GRID_W
    nrb = rows // NA_QROWS
    kc = kc_ref[...]
    vc = vc_ref[...]
    for sub in range(NA_SUB):
        rb = pl.program_id(2) * NA_SUB + sub
        rs = jnp.clip(rb * NA_QROWS - NA_WIN_R // 2, 0, rows - NA_BAND)
        start = pl.multiple_of(rs * GRID_W, GRID_W)
        case = jnp.where(rb == 0, 0, jnp.where(rb == nrb - 1, 2, 1))
        kw = k_ref[pl.ds(start, nkeys), :]
        vw = v_ref[pl.ds(start, nkeys), :]
        q2 = _stack_heads(q_ref[sub * tq:(sub + 1) * tq, :])
        s1 = _dot_nt(q2, kw) + b_ref[case, 0]
        s2 = _dot_nt(q2, kc)
        m = jnp.maximum(jnp.max(s1, axis=-1, keepdims=True), jnp.max(s2, axis=-1, keepdims=True))
        p1 = jnp.exp2(s1 - m)
        p2 = jnp.exp2(s2 - m)
        l = jnp.sum(p1, axis=-1, keepdims=True) + jnp.sum(p2, axis=-1, keepdims=True)
        o2 = _dot(p1.astype(BF16), vw) + _dot(p2.astype(BF16), vc)
        o_ref[sub * tq:(sub + 1) * tq, :] = _unstack_heads(o2 / l).astype(BF16)


def _na_x(z, zc, bias, *, nb, seq, lc):
    rows = seq // GRID_W
    assert rows % (NA_QROWS * NA_SUB) == 0 and rows >= NA_BAND
    nsteps = rows // (NA_QROWS * NA_SUB)
    tq = NA_QROWS * GRID_W
    return pl.pallas_call(
        functools.partial(_na_x_kernel, rows=rows),
        grid=(nb, NA_HEADS // 2, nsteps),
        in_specs=[
            pl.BlockSpec((NA_SUB * tq, 128), lambda b, hp, st: (b * nsteps + st, ZB_NA_Q + hp)),
            pl.BlockSpec((seq, 128), lambda b, hp, st: (b, ZB_NA_K + hp)),
            pl.BlockSpec((seq, 128), lambda b, hp, st: (b, ZB_NA_V + hp)),
            pl.BlockSpec((lc, 128), lambda b, hp, st: (b, ZB_NA_K + hp)),
            pl.BlockSpec((lc, 128), lambda b, hp, st: (b, ZB_NA_V + hp)),
            pl.BlockSpec((3, 1, 2 * tq, NA_BAND * GRID_W), lambda b, hp, st: (0, hp, 0, 0)),
        ],
        out_specs=pl.BlockSpec((NA_SUB * tq, 128), lambda b, hp, st: (b * nsteps + st, hp)),
        out_shape=jax.ShapeDtypeStruct((nb * seq, W_GRP), BF16),
        compiler_params=_cparams(("parallel", "parallel", "arbitrary")),
        name="na_attn",
    )(z, z, z, zc, zc, bias)


def _bias_table_kernel(v_ref, o_ref):
    qc = lax.broadcasted_iota(jnp.int32, (GRID_W, 128), 0)
    kc = lax.broadcasted_iota(jnp.int32, (GRID_W, 128), 1) % GRID_W
    c0 = jnp.clip(qc - NA_WIN_C // 2, 0, GRID_W - NA_WIN_C)
    col_ok = jnp.logical_and(kc >= c0, kc < c0 + NA_WIN_C)
    npairs = NA_BAND // 2
    for hl in range(2):
        for qr in range(NA_QROWS):
            for w in range(npairs):
                i = qr * npairs + w
                v = jnp.broadcast_to(v_ref[0, hl, i:i + 1, :], (GRID_W, 128))
                x = pltpu.roll(v, 0, 1, stride=1, stride_axis=0)
                r0 = (hl * NA_QROWS + qr) * GRID_W
                o_ref[0, 0, r0:r0 + GRID_W, w * 128:(w + 1) * 128] = jnp.where(col_ok, x, NEG_BIG)


def _na_bias_table(rpb):
    h = rpb.shape[0]
    nq = NA_QROWS * GRID_W
    nk = NA_BAND * GRID_W
    nrel = 2 * NA_WIN_C - 1
    npairs = NA_BAND // 2
    d = np.arange(nrel) - (NA_WIN_C - 1)
    place = np.zeros((2, nrel, 128), np.float32)
    place[0, np.arange(nrel), d % 128] = 1.0
    place[1, np.arange(nrel), GRID_W + d] = 1.0
    support = place.sum(axis=1)
    qr, wr = np.arange(NA_QROWS), np.arange(NA_BAND)
    half = NA_WIN_R // 2
    cases = ((0, np.zeros(NA_QROWS, int)), (half, np.arange(NA_QROWS)), (NA_WIN_R, np.full(NA_QROWS, half)))
    ridx, row_ok = [], []
    for dq, r0 in cases:
        row_ok.append((wr[None, :] >= r0[:, None]) & (wr[None, :] < r0[:, None] + NA_WIN_R))
        ridx.append(np.clip(wr[None, :] - (dq + qr[:, None]) + NA_WIN_R - 1, 0, 2 * NA_WIN_R - 2))
    ridx = np.stack(ridx).reshape(3, NA_QROWS, npairs, 2)
    row_ok = np.stack(row_ok).reshape(3, NA_QROWS, npairs, 2)
    rows = rpb[:, ridx, :] * LOG2E
    vec = jnp.einsum("hbqwpc,pcl->bhqwpl", rows, jnp.asarray(place), precision=lax.Precision.HIGHEST)
    masked = jnp.asarray(NEG_BIG * support)
    vec = jnp.where(jnp.asarray(row_ok)[:, None, :, :, :, None], vec, masked).sum(axis=4)
    vec = vec.reshape(3, h, NA_QROWS * npairs, 128)
    return pl.pallas_call(
        _bias_table_kernel,
        grid=(3, h // 2),
        in_specs=[pl.BlockSpec((1, 2, NA_QROWS * npairs, 128), lambda b, hp: (b, hp, 0, 0))],
        out_specs=pl.BlockSpec((1, 1, 2 * nq, nk), lambda b, hp: (b, hp, 0, 0)),
        out_shape=jax.ShapeDtypeStruct((3, h // 2, 2 * nq, nk), F32),
        compiler_params=_cparams(("parallel", "parallel")),
        name="na_bias_table",
    )(vec)


def _na_ctx_kernel(q_ref, k_ref, v_ref, o_ref):
    q2 = _stack_heads(q_ref[...])
    s = _dot_nt(q2, k_ref[...])
    m = jnp.max(s, axis=-1, keepdims=True)
    p = jnp.exp2(s - m)
    l = jnp.sum(p, axis=-1, keepdims=True)
    o2 = _dot(p.astype(BF16), v_ref[...])
    o_ref[...] = _unstack_heads(o2 / l).astype(BF16)


def _na_ctx(zc, *, nb, lc):
    return pl.pallas_call(
        _na_ctx_kernel,
        grid=(nb, NA_HEADS // 2),
        in_specs=[
            pl.BlockSpec((lc, 128), lambda b, hp: (b, ZB_NA_Q + hp)),
            pl.BlockSpec((lc, 128), lambda b, hp: (b, ZB_NA_K + hp)),
            pl.BlockSpec((lc, 128), lambda b, hp: (b, ZB_NA_V + hp)),
        ],
        out_specs=pl.BlockSpec((lc, 128), lambda b, hp: (b, hp)),
        out_shape=jax.ShapeDtypeStruct((nb * lc, W_GRP), BF16),
        compiler_params=_cparams(("parallel", "parallel")),
        name="na_attn_ctx",
    )(zc, zc, zc)


def _ta_kernel(*refs, tk, nk, has_extra):
    if has_extra:
        q_ref, k_ref, v_ref, ke_ref, ve_ref, o_ref = refs
    else:
        q_ref, k_ref, v_ref, o_ref = refs
    q = q_ref[...]
    tq = q.shape[0]

    def step(kb, vb, carry):
        m, acc = carry
        s = _dot_nt(q, kb)
        mn = jnp.maximum(m, jnp.max(s, axis=-1, keepdims=True))
        p = jnp.exp2(s - mn).astype(BF16)
        acc = jnp.exp2(m - mn) * acc + _dot(p, vb)
        return mn, acc

    def body(c, carry):
        st = pl.multiple_of(c * tk, tk)
        return step(k_ref[pl.ds(st, tk), :], v_ref[pl.ds(st, tk), :], carry)

    carry = (jnp.full((tq, 1), NEG_BIG, F32), jnp.zeros((tq, 128), F32))
    carry = lax.fori_loop(0, nk, body, carry, unroll=8 if nk % 8 == 0 else 1)
    if has_extra:
        carry = step(ke_ref[...], ve_ref[...], carry)
    _, acc = carry
    lane = lax.broadcasted_iota(jnp.int32, acc.shape, 1)
    l = jnp.sum(jnp.where(lane == TA_DIM, acc, 0.0), axis=-1, keepdims=True)
    o_ref[...] = jnp.where(lane < TA_DIM, acc / l, 0.0).astype(BF16)


def _ta(zq, zextra, *, nb, lq, le):
    tq = min(1024, lq)
    tk = min(1024, lq)
    nqb = lq // tq
    has_extra = zextra is not None
    in_specs = [
        pl.BlockSpec((tq, 128), lambda b, qi: (b * nqb + qi, ZB_TA_Q)),
        pl.BlockSpec((lq, 128), lambda b, qi: (b, ZB_TA_K)),
        pl.BlockSpec((lq, 128), lambda b, qi: (b, ZB_TA_V)),
    ]
    args = [zq, zq, zq]
    if has_extra:
        in_specs += [
            pl.BlockSpec((le, 128), lambda b, qi: (b, ZB_TA_K)),
            pl.BlockSpec((le, 128), lambda b, qi: (b, ZB_TA_V)),
        ]
        args += [zextra, zextra]
    return pl.pallas_call(
        functools.partial(_ta_kernel, tk=tk, nk=lq // tk, has_extra=has_extra),
        grid=(nb, nqb),
        in_specs=in_specs,
        out_specs=pl.BlockSpec((tq, 128), lambda b, qi: (b * nqb + qi, 0)),
        out_shape=jax.ShapeDtypeStruct((nb * lq, 128), BF16),
        compiler_params=_cparams(("parallel", "arbitrary")),
        name="ta_attn",
    )(*args)


def _sg_kernel(u_ref, gv_ref, tao_ref, gain_ref, wcat_ref, bsx_ref, two_ref, o_ref, v_scr):
    tm = u_ref.shape[0]
    gv = gv_ref[...].astype(F32)
    ms = jnp.mean(gv * gv, axis=-1, keepdims=True)
    v_scr[...] = (gv * lax.rsqrt(ms + EPS) * gain_ref[...]).astype(BF16)
    lane = lax.broadcasted_iota(jnp.int32, (SG_CHUNK, 128), 1)
    zero = jnp.zeros((SG_CHUNK, 128), BF16)

    def chunk(c, carry):
        r0 = pl.multiple_of(c * SG_CHUNK, SG_CHUNK)
        rows = pl.ds(r0, SG_CHUNK)
        outs = []
        for p in range(SG_HEADS // 2):
            vs = v_scr[rows, p * 128:(p + 1) * 128]
            rhs = jnp.concatenate([jnp.where(lane < 64, vs, zero), jnp.where(lane >= 64, vs, zero)], axis=0)
            outs.append(_dot(wcat_ref[p], rhs))
        s = jnp.concatenate(outs, axis=1) + bsx_ref[...] + _dot(tao_ref[rows, :], two_ref[...])
        o_ref[rows, :] = (u_ref[rows, :].astype(F32) * s).astype(BF16)
        return carry

    lax.fori_loop(0, tm // SG_CHUNK, chunk, 0)


def _sg(z, tao, gain, wcat, bsx, two, *, tpb):
    t = z.shape[0]
    tm = min(1024, tpb)
    const2 = lambda i: (0, 0)
    return pl.pallas_call(
        _sg_kernel,
        grid=(t // tm,),
        in_specs=[
            pl.BlockSpec((tm, W_GRP), lambda i: (i, ZT_SG_U)),
            pl.BlockSpec((tm, W_GRP), lambda i: (i, ZT_SG_V)),
            pl.BlockSpec((tm, 128), lambda i: (i, 0)),
            pl.BlockSpec((1, W_GRP), const2),
            pl.BlockSpec((SG_HEADS // 2, SG_CHUNK, 2 * SG_CHUNK), lambda i: (0, 0, 0)),
            pl.BlockSpec((SG_CHUNK, W_GRP), const2),
            pl.BlockSpec((128, W_GRP), const2),
        ],
        out_specs=pl.BlockSpec((tm, W_GRP), lambda i: (i, 0)),
        out_shape=jax.ShapeDtypeStruct((t, W_GRP), BF16),
        scratch_shapes=[pltpu.VMEM((tm, W_GRP), BF16)],
        compiler_params=_cparams(("parallel",)),
        name="sg_unit",
    )(z, z, tao, gain, wcat, bsx, two)


def _halo_specs(tm, t, width, col):
    hb = tm // HALO
    last = t // HALO - 1
    return [
        pl.BlockSpec((tm, width), lambda i: (i, col)),
        pl.BlockSpec((HALO, width), lambda i: (jnp.maximum(i * hb - 1, 0), col)),
        pl.BlockSpec((HALO, width), lambda i: (jnp.minimum((i + 1) * hb, last), col)),
    ]


def _fill_ext(ext_scr, main_ref, prev_ref, next_ref, it, tiles_pb):
    tm = main_ref.shape[0]
    zero = jnp.zeros(prev_ref.shape, F32)
    ext_scr[0:HALO, :] = jnp.where(it == 0, zero, prev_ref[...].astype(F32))
    ext_scr[HALO:HALO + tm, :] = main_ref[...].astype(F32)
    ext_scr[HALO + tm:2 * HALO + tm, :] = jnp.where(it == tiles_pb - 1, zero, next_ref[...].astype(F32))


def _pool_kernel(z_ref, zp_ref, zn_ref, wbd_ref, scale_ref, o_ref, ext_scr, d_scr, *, tiles_pb, seq):
    tm = z_ref.shape[0]
    it = pl.program_id(0) % tiles_pb
    _fill_ext(ext_scr, z_ref, zp_ref, zn_ref, it, tiles_pb)
    rc = 128

    for c in range(tm // rc):
        r0 = c * rc
        pos = it * tm + r0 + lax.broadcasted_iota(jnp.int32, (rc, 1), 0)
        for g, w in enumerate(PL_WINDOWS):
            cols = slice(g * 128, (g + 1) * 128)
            lo = r0 + HALO - w // 2
            acc = ext_scr[lo:lo + rc, cols]
            for s in range(1, w):
                acc = acc + ext_scr[lo + s:lo + s + rc, cols]
            cnt = (jnp.minimum(pos + w // 2, seq) - jnp.maximum(pos - w // 2, 0)).astype(F32)
            d = acc / cnt - ext_scr[r0 + HALO:r0 + HALO + rc, cols]
            d_scr[r0:r0 + rc, cols] = d.astype(BF16)
    o_ref[...] = (_dot(d_scr[...], wbd_ref[...]) * scale_ref[...]).astype(BF16)


def _pool(z, wbd, scale, *, tpb):
    t = z.shape[0]
    tm = min(1024, tpb)
    const2 = lambda i: (0, 0)
    return pl.pallas_call(
        functools.partial(_pool_kernel, tiles_pb=tpb // tm, seq=tpb),
        grid=(t // tm,),
        in_specs=_halo_specs(tm, t, W_GRP, ZT_POOL) + [
            pl.BlockSpec((W_GRP, W_GRP), const2),
            pl.BlockSpec((1, W_GRP), const2),
        ],
        out_specs=pl.BlockSpec((tm, W_GRP), lambda i: (i, 0)),
        out_shape=jax.ShapeDtypeStruct((t, W_GRP), BF16),
        scratch_shapes=[pltpu.VMEM((tm + 2 * HALO, W_GRP), F32), pltpu.VMEM((tm, W_GRP), BF16)],
        compiler_params=_cparams(("parallel",)),
        name="pool_mixer",
    )(z, z, z, wbd, scale)


def _conv_kernel(h_ref, hp_ref, hn_ref, dww_ref, dwb_ref, ng_ref, pww_ref, pwb_ref, o_ref, ext_scr, sh_scr,
                 a_scr, *, tiles_pb):
    tm = h_ref.shape[0]
    it = pl.program_id(0) % tiles_pb
    _fill_ext(ext_scr, h_ref, hp_ref, hn_ref, it, tiles_pb)
    rc = 64
    base = HALO - (CV_KERNEL - 1) // 2
    nsh = sh_scr.shape[1]
    for s in range(1, SUBLANES):
        sh_scr[s - 1] = ext_scr[s:s + nsh, :]

    for c in range(tm // rc):
        r0 = c * rc
        acc = jnp.broadcast_to(dwb_ref[...], (rc, W_GRP))
        for k in range(CV_KERNEL):
            a, s = divmod(base + k, SUBLANES)
            lo = r0 + a * SUBLANES
            rows = ext_scr[lo:lo + rc, :] if s == 0 else sh_scr[s - 1, lo:lo + rc, :]
            acc = acc + dww_ref[k:k + 1, :] * rows
        ms = jnp.mean(acc * acc, axis=-1, keepdims=True)
        hh = acc * lax.rsqrt(ms + EPS) * ng_ref[...]
        a_scr[r0:r0 + rc, :] = (hh * jax.nn.sigmoid(hh)).astype(BF16)
    o_ref[...] = (_dot(a_scr[...], pww_ref[...]) + pwb_ref[...]).astype(BF16)


def _conv(z, dww, dwb, ng, pww, pwb, *, tpb):
    t = z.shape[0]
    tm = min(512, tpb)
    const2 = lambda i: (0, 0)
    return pl.pallas_call(
        functools.partial(_conv_kernel, tiles_pb=tpb // tm),
        grid=(t // tm,),
        in_specs=_halo_specs(tm, t, W_GRP, ZT_CONV) + [
            pl.BlockSpec((CV_KERNEL, W_GRP), const2),
            pl.BlockSpec((1, W_GRP), const2),
            pl.BlockSpec((1, W_GRP), const2),
            pl.BlockSpec((W_GRP, W_GRP), const2),
            pl.BlockSpec((1, W_GRP), const2),
        ],
        out_specs=pl.BlockSpec((tm, W_GRP), lambda i: (i, 0)),
        out_shape=jax.ShapeDtypeStruct((t, W_GRP), BF16),
        scratch_shapes=[pltpu.VMEM((tm + 2 * HALO, W_GRP), F32),
                        pltpu.VMEM((SUBLANES - 1, tm + 2 * HALO - SUBLANES, W_GRP), F32),
                        pltpu.VMEM((tm, W_GRP), BF16)],
        compiler_params=_cparams(("parallel",)),
        name="conformer_conv",
    )(z, z, z, dww, dwb, ng, pww, pwb)


def _outproj_kernel(yna_ref, ysg_ref, ypl_ref, ycv_ref, w_ref, x_ref, g1_ref, sh2_ref, sc2_ref, x1_ref, hn_ref):
    tm, d = x_ref.shape
    y = jnp.concatenate([yna_ref[...], ysg_ref[...], ypl_ref[...], ycv_ref[...]], axis=1)
    nc = 512
    ssq = jnp.zeros((tm, 1), F32)
    for n in range(d // nc):
        cs = slice(n * nc, (n + 1) * nc)
        x1 = x_ref[:, cs] + g1_ref[0][:, cs] * _dot(y, w_ref[:, cs])
        x1_ref[:, cs] = x1
        ssq = ssq + jnp.sum(x1 * x1, axis=-1, keepdims=True)
    r = lax.rsqrt(ssq * (1.0 / d) + EPS)
    for n in range(d // nc):
        cs = slice(n * nc, (n + 1) * nc)
        hn_ref[:, cs] = (x1_ref[:, cs] * r * (1.0 + sc2_ref[0][:, cs]) + sh2_ref[0][:, cs]).astype(BF16)


def _outproj(ys, w_out, layer, x2d, g1, sh2, sc2, *, tpb):
    t, d = x2d.shape
    tm = min(512, tpb)
    tiles_pb = tpb // tm
    row = lambda i: (i // tiles_pb, 0, 0)
    return pl.pallas_call(
        _outproj_kernel,
        grid=(t // tm,),
        in_specs=[pl.BlockSpec((tm, W_GRP), lambda i: (i, 0))] * 4 + [
            pl.BlockSpec((None, d, d), lambda i: (layer, 0, 0)),
            pl.BlockSpec((tm, d), lambda i: (i, 0)),
            pl.BlockSpec((1, 1, d), row),
            pl.BlockSpec((1, 1, d), row),
            pl.BlockSpec((1, 1, d), row),
        ],
        out_specs=[pl.BlockSpec((tm, d), lambda i: (i, 0)), pl.BlockSpec((tm, d), lambda i: (i, 0))],
        out_shape=[jax.ShapeDtypeStruct((t, d), F32), jax.ShapeDtypeStruct((t, d), BF16)],
        compiler_params=_cparams(("parallel",)),
        name="out_proj",
    )(*ys, w_out, x2d, g1, sh2, sc2)


def _ffn_kernel(hn_ref, hp_ref, hx_ref, x1_ref, g2_ref, wg_ref, wv_ref, wd_ref, dw_ref, o_ref, ext_scr, g_scr,
                *, tiles_pb, nx):
    tm, d = hn_ref.shape
    j = pl.program_id(1)
    it = pl.program_id(0) % tiles_pb

    @pl.when(j == 0)
    def _():
        zero = jnp.zeros((HALO, d), BF16)
        ext_scr[0:HALO, :] = jnp.where(it == 0, zero, hp_ref[...])
        ext_scr[HALO:HALO + tm, :] = hn_ref[...]
        ext_scr[HALO + tm:2 * HALO + tm, :] = jnp.where(it == tiles_pb - 1, zero, hx_ref[...])
        o_ref[...] = jnp.zeros(o_ref.shape, F32)

    g_scr[...] = _dot(ext_scr[...], wg_ref[...])
    val = _dot(hn_ref[...], wv_ref[...])
    c = (dw_ref[0:1, :] * g_scr[HALO - 1:HALO - 1 + tm, :] + dw_ref[1:2, :] * g_scr[HALO:HALO + tm, :]
         + dw_ref[2:3, :] * g_scr[HALO + 1:HALO + 1 + tm, :])
    a = (c * jax.nn.sigmoid(c) * val).astype(BF16)
    nc = 512
    for n in range(d // nc):
        cs = slice(n * nc, (n + 1) * nc)
        o_ref[:, cs] += g2_ref[0][:, cs] * _dot(a, wd_ref[:, cs])

    xc = d // nx
    for s in range(nx):
        @pl.when(j == s)
        def _(s=s):
            o_ref[:, s * xc:(s + 1) * xc] += x1_ref[...]


def _ffn(hn, x1, g2, w_up, wd, layer, dw, *, tpb):
    t, d = x1.shape
    ff = wd.shape[1]
    tm = min(1024, tpb)
    tiles_pb = tpb // tm
    tf = 512
    nj = ff // tf
    nx = 8
    assert nj >= nx
    hb = tm // HALO
    last = t // HALO - 1
    return pl.pallas_call(
        functools.partial(_ffn_kernel, tiles_pb=tiles_pb, nx=nx),
        grid=(t // tm, nj),
        in_specs=[
            pl.BlockSpec((tm, d), lambda i, j: (i, 0)),
            pl.BlockSpec((HALO, d), lambda i, j: (jnp.maximum(i * hb - 1, 0), 0)),
            pl.BlockSpec((HALO, d), lambda i, j: (jnp.minimum((i + 1) * hb, last), 0)),
            pl.BlockSpec((tm, d // nx), lambda i, j: (i, jnp.minimum(j, nx - 1))),
            pl.BlockSpec((1, 1, d), lambda i, j: (i // tiles_pb, 0, 0)),
            pl.BlockSpec((None, d, tf), lambda i, j: (layer, 0, j)),
            pl.BlockSpec((None, d, tf), lambda i, j: (layer, 0, nj + j)),
            pl.BlockSpec((None, tf, d), lambda i, j: (layer, j, 0)),
            pl.BlockSpec((FF_KERNEL, tf), lambda i, j: (0, j)),
        ],
        out_specs=pl.BlockSpec((tm, d), lambda i, j: (i, 0)),
        out_shape=jax.ShapeDtypeStruct((t, d), F32),
        scratch_shapes=[pltpu.VMEM((tm + 2 * HALO, d), BF16), pltpu.VMEM((tm + 2 * HALO, tf), F32)],
        compiler_params=_cparams(("parallel", "arbitrary")),
        name="conv_glu_ffn",
    )(hn, hn, hn, x1, g2, w_up, w_up, wd, dw)


def _rope_tables(seq):
    half = TA_DIM // 2
    freqs = ROPE_BASE ** (-np.arange(0, half, 2, dtype=np.float32) / half)
    pos = np.arange(seq)
    lane = np.arange(TA_DIM)
    p = np.where(lane[None, :] < half, (pos // GRID_W)[:, None], (pos % GRID_W)[:, None]).astype(np.float32)
    m = lane % half
    ang = jnp.asarray(p * freqs[m % (half // 2)][None, :])
    sign = jnp.asarray(np.where(m < half // 2, -1.0, 1.0).astype(np.float32))[None, :]
    cos = jnp.concatenate([jnp.cos(ang), jnp.ones((seq, 128 - TA_DIM), F32)], axis=1)
    sin = jnp.concatenate([jnp.sin(ang) * sign, jnp.zeros((seq, 128 - TA_DIM), F32)], axis=1)
    return cos, sin


def _layout_w_in(w):
    lead = w.shape[:-1]
    g = W_GRP
    w = w.astype(BF16)
    o = 5 * g
    tq, tk, tv = (w[..., o + i * TA_DIM:o + (i + 1) * TA_DIM] for i in range(3))
    z64 = jnp.zeros(lead + (128 - TA_DIM,), BF16)
    tail = jnp.zeros(lead + (IN_TILE - 3 * 128 + 128 - TA_DIM,), BF16)
    return jnp.concatenate([w[..., :o], w[..., o + 3 * TA_DIM:], tq, z64, tk, z64, tv, tail], axis=-1)


def _layer_params(l, w_in, na_q_gain, na_k_gain, na_rpb, sg_v_gain, sg_ws, sg_bs, ta_q_gain, ta_k_gain,
                  ta_w_out, pl_w, pl_scale, cv_dw_w, cv_dw_b, cv_norm_gain, cv_pw_w, cv_pw_b, w_out,
                  ff_w_up, ff_dw_w, ff_w_down):
    ones64 = jnp.ones((128 - TA_DIM,), F32)
    ws = sg_ws[l].astype(BF16)
    ff = ff_w_down.shape[1]
    p = dict(
        gq=jnp.tile(na_q_gain[l], NA_HEADS)[None, :],
        gk=jnp.tile(na_k_gain[l], NA_HEADS)[None, :],
        gt=jnp.concatenate([ta_q_gain[l], ones64, ta_k_gain[l], ones64])[None, :],
        na_bias=_na_bias_table(na_rpb[l]),
        sg_gain=sg_v_gain[l][None, :],
        sg_wcat=jnp.concatenate([ws[0::2], ws[1::2]], axis=2),
        sg_bsx=jnp.repeat(sg_bs[l].T, W_GRP // SG_HEADS, axis=1),
        ta_w_out=jnp.concatenate([ta_w_out[l], jnp.zeros((128 - TA_DIM, W_GRP), F32)], axis=0).astype(BF16),
        pl_wbd=jax.scipy.linalg.block_diag(*[pl_w[l, g] for g in range(len(PL_WINDOWS))]).astype(BF16),
        pl_scale=pl_scale[l][None, :],
        cv_dw_w=cv_dw_w[l], cv_dw_b=cv_dw_b[l][None, :], cv_ng=cv_norm_gain[l][None, :],
        cv_pw_w=cv_pw_w[l].astype(BF16), cv_pw_b=cv_pw_b[l][None, :],
        ff_dw=ff_dw_w[l],
    )
    return p


def _smat():
    idx = np.arange(IN_TILE) // HEAD_DIM
    return jnp.asarray((idx[:, None] == np.arange(128)[None, :]).astype(np.float32)).astype(BF16)


def _mixers(p, z, na_o, ta_o, *, tpb):
    sg_o = _sg(z, ta_o, p["sg_gain"], p["sg_wcat"], p["sg_bsx"], p["ta_w_out"], tpb=tpb)
    pl_o = _pool(z, p["pl_wbd"], p["pl_scale"], tpb=tpb)
    cv_o = _conv(z, p["cv_dw_w"], p["cv_dw_b"], p["cv_ng"], p["cv_pw_w"], p["cv_pw_b"], tpb=tpb)
    return [na_o, sg_o, pl_o, cv_o]


def _tail(p, stacks, layer, ys, x2d, mods, *, tpb):
    w_out, ff_w_up, ff_wd = stacks
    x1, hn = _outproj(ys, w_out, layer, x2d, mods[2], mods[3], mods[4], tpb=tpb)
    return _ffn(hn, x1, mods[5], ff_w_up, ff_wd, layer, p["ff_dw"], tpb=tpb)


def kernel(x, c, ctx, c_ctx, w_mod, b_mod, w_in, na_q_gain, na_k_gain, na_rpb, sg_v_gain, sg_ws, sg_bs,
           ta_q_gain, ta_k_gain, ta_w_out, pl_w, pl_scale, cv_dw_w, cv_dw_b, cv_norm_gain, cv_pw_w, cv_pw_b,
           w_out, ff_w_up, ff_dw_w, ff_w_down):
    nb, seq, d = x.shape
    lc = ctx.shape[1]
    depth = w_mod.shape[0]
    mods_all = _mod_vectors(c, c_ctx, w_mod, b_mod).reshape(depth, 8, 6, d)
    smat = _smat()
    stacks = (w_out.astype(BF16), ff_w_up.astype(BF16), ff_w_down.astype(BF16))
    w_in_r = _layout_w_in(w_in)
    cos_x, sin_x = _rope_tables(seq)
    cos_c, sin_c = jnp.ones((lc, 128), F32), jnp.zeros((lc, 128), F32)

    x2d = x.reshape(nb * seq, d)
    xc2d = ctx.reshape(nb * lc, d)
    for l in range(depth):
        last = l == depth - 1
        p = _layer_params(l, w_in, na_q_gain, na_k_gain, na_rpb, sg_v_gain, sg_ws, sg_bs, ta_q_gain, ta_k_gain,
                          ta_w_out, pl_w, pl_scale, cv_dw_w, cv_dw_b, cv_norm_gain, cv_pw_w, cv_pw_b, w_out,
                          ff_w_up, ff_dw_w, ff_w_down)
        mx = [mods_all[l, :nb, k][:, None, :] for k in range(6)]
        mc = [jnp.broadcast_to(mods_all[l, nb, k][None, None, :], (nb, 1, d)) for k in range(6)]
        zc = _inproj(xc2d, mc[0], mc[1], w_in_r, l, smat, p["gq"], p["gk"], p["gt"], cos_c, sin_c, tpb=lc)
        z = _inproj(x2d, mx[0], mx[1], w_in_r, l, smat, p["gq"], p["gk"], p["gt"], cos_x, sin_x, tpb=seq)
        na_o = _na_x(z, zc, p["na_bias"], nb=nb, seq=seq, lc=lc)
        ta_o = _ta(z, zc, nb=nb, lq=seq, le=lc)
        ys = _mixers(p, z, na_o, ta_o, tpb=seq)
        if not last:
            na_oc = _na_ctx(zc, nb=nb, lc=lc)
            ta_oc = _ta(zc, None, nb=nb, lq=lc, le=0)
            ysc = _mixers(p, zc, na_oc, ta_oc, tpb=lc)
            xc2d = _tail(p, stacks, l, ysc, xc2d, mc, tpb=lc)
        x2d = _tail(p, stacks, l, ys, x2d, mx, tpb=seq)
    return x2d.reshape(nb, seq, d)
```
